```python
import math
import jax, jax.numpy as jnp
from jax import lax
import numpy as np

D_MODEL = 1024
BATCH = 16
SEQ = 2048
DEPTH = 4

D_MIX = D_MODEL
D_SSM = D_MIX // 2
D_ATT = D_MIX - D_SSM
SSM_GROUP = 16
SSM_GROUPS = D_SSM // SSM_GROUP
SSM_STATE = 64
DIFF_HEAD = 64
N_DIFF_HEADS = D_ATT // (2 * DIFF_HEAD)
DIFF_VDIM = 2 * DIFF_HEAD
ROT_DIM = DIFF_HEAD // 4
ROPE_THETA = 500000.0
Q_BLOCK = 128
D_FF = -(-8 * D_MODEL // (3 * 256)) * 256
IN_COLS = D_SSM + 3 * D_ATT
DEEPNORM_ALPHA = (2 * DEPTH) ** 0.25
DEEPNORM_BETA = (8 * DEPTH) ** -0.25
LN_EPS = 1e-5
RMS_EPS = 1e-5

kernel_name = "hybrid_s5_diffattn_deepnorm_adaln"


def layer_norm(x, g, b):
    x32 = x.astype(jnp.float32)
    mu = jnp.mean(x32, axis=-1, keepdims=True)
    xc = x32 - mu
    var = jnp.mean(xc * xc, axis=-1, keepdims=True)
    y = xc * lax.rsqrt(var + LN_EPS) * g.astype(jnp.float32) + b.astype(jnp.float32)
    return y.astype(x.dtype)


def apply_rotary(t, cos, sin):
    t_rot, t_pass = t[..., :ROT_DIM], t[..., ROT_DIM:]
    t1, t2 = t_rot[..., :ROT_DIM // 2], t_rot[..., ROT_DIM // 2:]
    rotated = jnp.concatenate([t1 * cos - t2 * sin, t2 * cos + t1 * sin], axis=-1)
    return jnp.concatenate([rotated.astype(t.dtype), t_pass], axis=-1)


def s5_mixer(u, a_re, a_im, log_step, b_re, b_im, c_re, c_im, d_skip, glu_w, glu_b):
    bsz, seqlen = u.shape[0], u.shape[1]
    f32 = jnp.float32
    u32 = u.astype(f32).reshape(bsz, seqlen, SSM_GROUPS, SSM_GROUP)
    delta = jnp.exp(log_step.astype(f32))[:, None]
    lam_re = jnp.minimum(a_re.astype(f32), -1e-4)
    lam_im = a_im.astype(f32)
    mag = jnp.exp(lam_re * delta)
    ang = lam_im * delta
    lb_re, lb_im = mag * jnp.cos(ang), mag * jnp.sin(ang)
    den = lam_re * lam_re + lam_im * lam_im
    n_re, n_im = lb_re - 1.0, lb_im
    k_re = (n_re * lam_re + n_im * lam_im) / den
    k_im = (n_im * lam_re - n_re * lam_im) / den
    b_re32, b_im32 = b_re.astype(f32), b_im.astype(f32)
    bb_re = k_re[..., None] * b_re32 - k_im[..., None] * b_im32
    bb_im = k_re[..., None] * b_im32 + k_im[..., None] * b_re32
    bu_re = jnp.einsum('blgc,gpc->blgp', u32, bb_re)
    bu_im = jnp.einsum('blgc,gpc->blgp', u32, bb_im)
    a_t_re = jnp.broadcast_to(lb_re, (1, seqlen, SSM_GROUPS, SSM_STATE))
    a_t_im = jnp.broadcast_to(lb_im, (1, seqlen, SSM_GROUPS, SSM_STATE))

    def combine(left, right):
        a1r, a1i, b1r, b1i = left
        a2r, a2i, b2r, b2i = right
        ar = a2r * a1r - a2i * a1i
        ai = a2r * a1i + a2i * a1r
        br = a2r * b1r - a2i * b1i + b2r
        bi = a2r * b1i + a2i * b1r + b2i
        return (ar, ai, br, bi)

    _, _, st_re, st_im = lax.associative_scan(combine, (a_t_re, a_t_im, bu_re, bu_im), axis=1)
    y = (jnp.einsum('blgp,gcp->blgc', st_re, c_re.astype(f32))
         - jnp.einsum('blgp,gcp->blgc', st_im, c_im.astype(f32))
         + d_skip.astype(f32) * u32)
    y = jax.nn.gelu(y).reshape(bsz, seqlen, D_SSM)
    y = y * jax.nn.sigmoid(y @ glu_w.astype(f32) + glu_b.astype(f32))
    return y.astype(u.dtype)


def diff_attention(q, k, v, lam, lam_init, subln_w):
    bsz, seqlen = q.shape[0], q.shape[1]
    f32 = jnp.float32
    n_blocks = seqlen // Q_BLOCK
    q = q * (DIFF_HEAD ** -0.5)
    qb = q.reshape(bsz, n_blocks, Q_BLOCK, N_DIFF_HEADS, 2, DIFF_HEAD).transpose(1, 0, 2, 3, 4, 5)
    v32 = v.astype(f32)
    key_idx = jnp.arange(seqlen)
    gain = subln_w.astype(f32) * (1.0 - lam_init)

    def block(args):
        q_blk, i = args
        s = jnp.einsum('bqhmd,bkhmd->bhmqk', q_blk, k).astype(f32)
        q_idx = i * Q_BLOCK + jnp.arange(Q_BLOCK)
        mask = key_idx[None, :] <= q_idx[:, None]
        s = jnp.where(mask, s, -jnp.inf)
        p = jax.nn.softmax(s, axis=-1)
        w = p[:, :, 0] - lam * p[:, :, 1]
        o = jnp.einsum('bhqk,bkhe->bqhe', w, v32)
        o = o * lax.rsqrt(jnp.mean(o * o, axis=-1, keepdims=True) + RMS_EPS) * gain
        return o

    out = lax.map(block, (qb, jnp.arange(n_blocks)))
    return out.transpose(1, 0, 2, 3, 4).reshape(bsz, seqlen, N_DIFF_HEADS * DIFF_VDIM)


def setup_inputs(seed: int = 0) -> dict:
    key = jax.random.key(seed)
    ks = jax.random.split(key, 32)
    f32 = jnp.float32

    def nrm(k, shape, scale):
        return jax.random.normal(k, shape, f32) * scale

    nl, G, P, C = DEPTH, SSM_GROUPS, SSM_STATE, SSM_GROUP
    x = nrm(ks[0], (BATCH, SEQ, D_MODEL), 1.0)
    c = nrm(ks[1], (BATCH, D_MODEL), 1.0)
    offset = jax.random.randint(ks[2], (BATCH, 1), 0, 4096, dtype=jnp.int32)
    positions = offset + jnp.arange(SEQ, dtype=jnp.int32)[None, :]
    mod_w = nrm(ks[3], (nl, D_MODEL, 6 * D_MODEL), 0.1 * D_MODEL ** -0.5)
    mod_b = nrm(ks[4], (nl, 6 * D_MODEL), 0.01)
    w_in = nrm(ks[5], (nl, D_MODEL, IN_COLS), D_MODEL ** -0.5)
    n_idx = jnp.arange(P, dtype=f32)
    ssm_a_re = -0.5 + nrm(ks[6], (nl, G, P), 0.01)
    ssm_a_im = math.pi * n_idx + nrm(ks[7], (nl, G, P), 0.01)
    ssm_log_step = jax.random.uniform(ks[8], (nl, G), f32, math.log(1e-3), math.log(1e-1))
    ssm_b_re = nrm(ks[9], (nl, G, P, C), (2 * C) ** -0.5)
    ssm_b_im = nrm(ks[10], (nl, G, P, C), (2 * C) ** -0.5)
    ssm_c_re = nrm(ks[11], (nl, G, C, P), P ** -0.5)
    ssm_c_im = nrm(ks[12], (nl, G, C, P), P ** -0.5)
    ssm_d = nrm(ks[13], (nl, G, C), 1.0)
    glu_w = nrm(ks[14], (nl, D_SSM, D_SSM), D_SSM ** -0.5)
    glu_b = nrm(ks[15], (nl, D_SSM), 0.01)
    lam_q1 = nrm(ks[16], (nl, DIFF_HEAD), 0.1)
    lam_k1 = nrm(ks[17], (nl, DIFF_HEAD), 0.1)
    lam_q2 = nrm(ks[18], (nl, DIFF_HEAD), 0.1)
    lam_k2 = nrm(ks[19], (nl, DIFF_HEAD), 0.1)
    subln_w = 1.0 + nrm(ks[20], (nl, DIFF_VDIM), 0.02)
    w_out = nrm(ks[21], (nl, D_MIX, D_MODEL), DEEPNORM_BETA * D_MIX ** -0.5)
    ln1_g = 1.0 + nrm(ks[22], (nl, D_MODEL), 0.02)
    ln1_b = nrm(ks[23], (nl, D_MODEL), 0.01)
    ffn_w_gate = nrm(ks[24], (nl, D_MODEL, D_FF), D_MODEL ** -0.5)
    ffn_w_up = nrm(ks[25], (nl, D_MODEL, D_FF), D_MODEL ** -0.5)
    ffn_w_down = nrm(ks[26], (nl, D_FF, D_MODEL), DEEPNORM_BETA * D_FF ** -0.5)
    ln2_g = 1.0 + nrm(ks[27], (nl, D_MODEL), 0.02)
    ln2_b = nrm(ks[28], (nl, D_MODEL), 0.01)
    return {"x": x, "c": c, "positions": positions,
            "mod_w": mod_w, "mod_b": mod_b, "w_in": w_in,
            "ssm_a_re": ssm_a_re, "ssm_a_im": ssm_a_im, "ssm_log_step": ssm_log_step,
            "ssm_b_re": ssm_b_re, "ssm_b_im": ssm_b_im, "ssm_c_re": ssm_c_re, "ssm_c_im": ssm_c_im,
            "ssm_d": ssm_d, "glu_w": glu_w, "glu_b": glu_b,
            "lam_q1": lam_q1, "lam_k1": lam_k1, "lam_q2": lam_q2, "lam_k2": lam_k2,
            "subln_w": subln_w, "w_out": w_out, "ln1_g": ln1_g, "ln1_b": ln1_b,
            "ffn_w_gate": ffn_w_gate, "ffn_w_up": ffn_w_up, "ffn_w_down": ffn_w_down,
            "ln2_g": ln2_g, "ln2_b": ln2_b}


def reference(x, c, positions, mod_w, mod_b, w_in, ssm_a_re, ssm_a_im, ssm_log_step,
              ssm_b_re, ssm_b_im, ssm_c_re, ssm_c_im, ssm_d, glu_w, glu_b,
              lam_q1, lam_k1, lam_q2, lam_k2, subln_w, w_out, ln1_g, ln1_b,
              ffn_w_gate, ffn_w_up, ffn_w_down, ln2_g, ln2_b):
    bsz, seqlen = x.shape[0], x.shape[1]
    f32 = jnp.float32
    cond = jax.nn.silu(c)
    freqs = ROPE_THETA ** (-jnp.arange(0, ROT_DIM, 2, dtype=f32) / ROT_DIM)
    angles = positions.astype(f32)[..., None] * freqs
    cos = jnp.cos(angles)[:, :, None, None, :]
    sin = jnp.sin(angles)[:, :, None, None, :]

    for l in range(DEPTH):
        lam_init = 0.8 - 0.6 * math.exp(-0.3 * l)
        mod = cond @ mod_w[l] + mod_b[l]
        shift1, scale1, gate1, shift2, scale2, gate2 = [m[:, None, :] for m in jnp.split(mod, 6, axis=-1)]

        h = x * (1.0 + scale1) + shift1
        proj = h @ w_in[l]
        u, q, k, v = jnp.split(proj, [D_SSM, D_SSM + D_ATT, D_SSM + 2 * D_ATT], axis=-1)
        ssm_out = s5_mixer(u, ssm_a_re[l], ssm_a_im[l], ssm_log_step[l], ssm_b_re[l], ssm_b_im[l],
                           ssm_c_re[l], ssm_c_im[l], ssm_d[l], glu_w[l], glu_b[l])
        q = apply_rotary(q.reshape(bsz, seqlen, N_DIFF_HEADS, 2, DIFF_HEAD), cos, sin)
        k = apply_rotary(k.reshape(bsz, seqlen, N_DIFF_HEADS, 2, DIFF_HEAD), cos, sin)
        v = v.reshape(bsz, seqlen, N_DIFF_HEADS, DIFF_VDIM)
        lam = (jnp.exp(jnp.sum(lam_q1[l].astype(f32) * lam_k1[l].astype(f32)))
               - jnp.exp(jnp.sum(lam_q2[l].astype(f32) * lam_k2[l].astype(f32))) + lam_init)
        att_out = diff_attention(q, k, v, lam, lam_init, subln_w[l])
        mix = jnp.concatenate([ssm_out, att_out.astype(x.dtype)], axis=-1) @ w_out[l]
        x = layer_norm(DEEPNORM_ALPHA * x + (1.0 + gate1) * mix, ln1_g[l], ln1_b[l])

        h = x * (1.0 + scale2) + shift2
        ffn = (jax.nn.silu(h @ ffn_w_gate[l]) * (h @ ffn_w_up[l])) @ ffn_w_down[l]
        x = layer_norm(DEEPNORM_ALPHA * x + (1.0 + gate2) * ffn, ln2_g[l], ln2_b[l])
    return x
```

```python
import functools
import math

import jax
import jax.numpy as jnp
from jax import lax
from jax.experimental import pallas as pl
from jax.experimental.pallas import tpu as pltpu

F32 = jnp.float32
BF16 = jnp.bfloat16

SSM_GROUP = 16
SSM_STATE = 64
DIFF_HEAD = 64
ROT_DIM = DIFF_HEAD // 4
ROPE_THETA = 500000.0
LN_EPS = 1e-5
RMS_EPS = 1e-5

LANES = 128
SUBLANES = 8
MXU_DIM = 256
VMEM_LIMIT_BYTES = 56 * 1024 * 1024

ROW_TILE = 512
SCAN_TILE = 256
SCAN_PITCH = SCAN_TILE + SUBLANES
ATT_Q_TILE = 256
ATT_K_TILE = 256


def _sigmoid(x):
    return 1.0 / (1.0 + jnp.exp(-x))


def _silu(x):
    return x * _sigmoid(x)


def _gelu_tanh(x):
    c = math.sqrt(2.0 / math.pi)
    return 0.5 * x * (1.0 + jnp.tanh(c * (x + 0.044715 * (x * x * x))))


def _dot(a, b):
    return jnp.dot(a, b, preferred_element_type=F32)


def _layer_norm(y, g, b):
    mu = jnp.mean(y, axis=-1, keepdims=True)
    yc = y - mu
    var = jnp.mean(yc * yc, axis=-1, keepdims=True)
    return yc * lax.rsqrt(var + LN_EPS) * g + b


def _params(*sem):
    return pltpu.CompilerParams(dimension_semantics=sem, vmem_limit_bytes=VMEM_LIMIT_BYTES)


def _mod_kernel(c_ref, w_ref, b_ref, o_ref):
    cond = _silu(c_ref[...])
    o_ref[...] = _dot(cond.astype(BF16), w_ref[...].astype(BF16)) + b_ref[...]


def _modulation(c, mod_w, mod_b):
    depth, d, six_d = mod_w.shape
    bsz = c.shape[0]
    n_col = six_d // d
    return pl.pallas_call(
        _mod_kernel,
        grid=(depth, n_col),
        in_specs=[
            pl.BlockSpec((bsz, d), lambda l, j: (0, 0)),
            pl.BlockSpec((pl.Squeezed(), d, d), lambda l, j: (l, 0, j)),
            pl.BlockSpec((pl.Squeezed(), 1, d), lambda l, j: (l, 0, j)),
        ],
        out_specs=pl.BlockSpec((pl.Squeezed(), bsz, d), lambda l, j: (l, 0, j)),
        out_shape=jax.ShapeDtypeStruct((depth, bsz, six_d), F32),
        compiler_params=_params("arbitrary", "arbitrary"),
        name="adaln_modulation",
    )(c, mod_w, mod_b.reshape(depth, 1, six_d))


def _rope_kernel(pos_ref, freq_ref, cos_ref, sina_ref, sinb_ref):
    pos = pos_ref[...].astype(F32)
    ang = pos * freq_ref[...]
    lane = lax.broadcasted_iota(jnp.int32, ang.shape, 1) % DIFF_HEAD
    c = jnp.cos(ang)
    s = jnp.sin(ang)
    half = ROT_DIM // 2
    cos_ref[...] = jnp.where(lane < ROT_DIM, c, 1.0)
    sina_ref[...] = jnp.where(lane < half, -s, 0.0)
    sinb_ref[...] = jnp.where((lane >= half) & (lane < ROT_DIM), s, 0.0)


def _rope_tables(positions):
    bsz, seqlen = positions.shape
    half = ROT_DIM // 2
    freqs = ROPE_THETA ** (-jnp.arange(0, ROT_DIM, 2, dtype=F32) / ROT_DIM)
    lane = jnp.arange(LANES) % DIFF_HEAD
    freq_lane = jnp.where(lane < ROT_DIM, freqs[lane % half], 0.0).reshape(1, LANES)
    rows = ROW_TILE
    spec = pl.BlockSpec((pl.Squeezed(), rows, LANES), lambda b, i: (b, i, 0))
    shape = jax.ShapeDtypeStruct((bsz, seqlen, LANES), F32)
    return pl.pallas_call(
        _rope_kernel,
        grid=(bsz, seqlen // rows),
        in_specs=[
            pl.BlockSpec((pl.Squeezed(), rows, 1), lambda b, i: (b, i, 0)),
            pl.BlockSpec((1, LANES), lambda b, i: (0, 0)),
        ],
        out_specs=[spec, spec, spec],
        out_shape=[shape, shape, shape],
        compiler_params=_params("arbitrary", "arbitrary"),
        name="rope_tables",
    )(positions.reshape(bsz, seqlen, 1), freq_lane)


def _disc_kernel(are_ref, aim_ref, ls_ref, bre_ref, bim_ref,
                 lbre_ref, lbim_ref, bbre_ref, bbim_ref):
    delta = jnp.exp(ls_ref[...])
    lam_re = jnp.minimum(are_ref[...], -1e-4)
    lam_im = aim_ref[...]
    mag = jnp.exp(lam_re * delta)
    ang = lam_im * delta
    lb_re = mag * jnp.cos(ang)
    lb_im = mag * jnp.sin(ang)
    den = lam_re * lam_re + lam_im * lam_im
    n_re = lb_re - 1.0
    n_im = lb_im
    k_re = (n_re * lam_re + n_im * lam_im) / den
    k_im = (n_im * lam_re - n_re * lam_im) / den
    b_re = bre_ref[...]
    b_im = bim_ref[...]
    lbre_ref[...] = lb_re
    lbim_ref[...] = lb_im
    bbre_ref[...] = k_re * b_re - k_im * b_im
    bbim_ref[...] = k_re * b_im + k_im * b_re


def _discretise(a_re, a_im, log_step, b_re, b_im):
    depth, g, p = a_re.shape
    c = b_re.shape[-1]
    rows = depth * g * c

    def rep(a):
        return jnp.broadcast_to(a[:, :, None, :], (depth, g, c, p)).reshape(rows, p)

    ls = jnp.broadcast_to(log_step[:, :, None, None], (depth, g, c, 1)).reshape(rows, 1)
    bt_re = b_re.transpose(0, 1, 3, 2).reshape(rows, p)
    bt_im = b_im.transpose(0, 1, 3, 2).reshape(rows, p)
    shape = jax.ShapeDtypeStruct((rows, p), F32)
    outs = pl.pallas_call(
        _disc_kernel,
        out_shape=[shape, shape, shape, shape],
        name="s5_discretise",
    )(rep(a_re), rep(a_im), ls, bt_re, bt_im)
    lb_re, lb_im, bb_re, bb_im = [o.reshape(depth, g, c, p) for o in outs]
    return lb_re[:, :, 0, :], lb_im[:, :, 0, :], bb_re, bb_im


def _block_diag_in(w):
    depth, g, c, p = w.shape
    gh = g // 2
    eye = jnp.eye(gh, dtype=w.dtype)
    w = w.reshape(depth, 2, gh, c, p)
    return jnp.einsum("lhgcp,gk->lhgckp", w, eye).reshape(depth, 2, gh * c, gh * p)


def _block_diag_out(w):
    depth, g, c, p = w.shape
    gh = g // 2
    eye = jnp.eye(gh, dtype=w.dtype)
    w = w.reshape(depth, 2, gh, c, p)
    return jnp.einsum("lhgcp,gk->lhgpkc", w, eye).reshape(depth, 2, gh * p, gh * c)


def _in_proj_kernel(x_ref, mod_ref, w_ref, cos_ref, sina_ref, sinb_ref,
                    u_ref, q_ref, k_ref, v_ref, *, d_ssm, d_att):
    shift = mod_ref[0:1, :]
    scale = mod_ref[1:2, :]
    h = x_ref[...] * (1.0 + scale) + shift
    proj = _dot(h.astype(BF16), w_ref[...])
    u_ref[...] = proj[:, :d_ssm]
    cos_t = cos_ref[...]
    sin_a = sina_ref[...]
    sin_b = sinb_ref[...]
    half = ROT_DIM // 2

    def rotary(t):
        return t * cos_t + pltpu.roll(t, LANES - half, 1) * sin_a + pltpu.roll(t, half, 1) * sin_b

    q_scale = DIFF_HEAD ** -0.5
    for blk in range(d_att // LANES):
        lo = d_ssm + blk * LANES
        q_ref[:, blk * LANES:(blk + 1) * LANES] = (rotary(proj[:, lo:lo + LANES]) * q_scale).astype(BF16)
        lo = d_ssm + d_att + blk * LANES
        k_ref[:, blk * LANES:(blk + 1) * LANES] = rotary(proj[:, lo:lo + LANES]).astype(BF16)
    v_ref[...] = proj[:, d_ssm + 2 * d_att:].astype(BF16)


def _in_proj(x, mod, w_in, ropes, d_ssm, d_att):
    bsz, seqlen, d = x.shape
    rows = ROW_TILE
    n_cols = w_in.shape[1]
    tok = lambda width: pl.BlockSpec((pl.Squeezed(), rows, width), lambda b, i: (b, i, 0))
    return pl.pallas_call(
        functools.partial(_in_proj_kernel, d_ssm=d_ssm, d_att=d_att),
        grid=(bsz, seqlen // rows),
        in_specs=[
            tok(d),
            pl.BlockSpec((pl.Squeezed(), mod.shape[1], d), lambda b, i: (b, 0, 0)),
            pl.BlockSpec((d, n_cols), lambda b, i: (0, 0)),
            tok(LANES), tok(LANES), tok(LANES),
        ],
        out_specs=[tok(d_ssm), tok(d_att), tok(d_att), tok(d_att)],
        out_shape=[
            jax.ShapeDtypeStruct((bsz, seqlen, d_ssm), F32),
            jax.ShapeDtypeStruct((bsz, seqlen, d_att), BF16),
            jax.ShapeDtypeStruct((bsz, seqlen, d_att), BF16),
            jax.ShapeDtypeStruct((bsz, seqlen, d_att), BF16),
        ],
        compiler_params=_params("arbitrary", "arbitrary"),
        name="in_proj_rotary",
    )(x, mod, w_in, *ropes)


def _s5_kernel(u_ref, lam_ref, wb_ref, wc_ref, d_ref, gw_ref, gb_ref, o_ref,
               slab_ref, carry_ref, *, n_slab):
    tile = u_ref.shape[0]
    pitch = SCAN_PITCH
    half_slabs = n_slab // 2
    quarter = half_slabs // 2

    @pl.when(pl.program_id(1) == 0)
    def _():
        carry_ref[...] = jnp.zeros_like(carry_ref)

    u = u_ref[...]
    ub = u.astype(BF16)
    half_w = ub.shape[1] // 2
    for part in range(2):
        for gh in range(2):
            r = _dot(ub[:, gh * half_w:(gh + 1) * half_w], wb_ref[part, gh])
            for s in range(quarter):
                slab = part * half_slabs + gh * quarter + s
                slab_ref[pl.ds(slab * pitch, tile), :] = r[:, s * LANES:(s + 1) * LANES]

    lam_re = [lam_ref[0, v * SUBLANES:(v + 1) * SUBLANES, :] for v in range(2)]
    lam_im = [lam_ref[1, v * SUBLANES:(v + 1) * SUBLANES, :] for v in range(2)]

    def rows(t, part, v):
        first = (part * half_slabs + v * SUBLANES) * pitch
        return pl.ds(first + t, SUBLANES, stride=pitch)

    def step(t, state):
        new = []
        for v in range(2):
            x_re, x_im = state[2 * v], state[2 * v + 1]
            b_re = slab_ref[rows(t, 0, v), :]
            b_im = slab_ref[rows(t, 1, v), :]
            n_re = lam_re[v] * x_re - lam_im[v] * x_im + b_re
            n_im = lam_re[v] * x_im + lam_im[v] * x_re + b_im
            slab_ref[rows(t, 0, v), :] = n_re
            slab_ref[rows(t, 1, v), :] = n_im
            new += [n_re, n_im]
        return tuple(new)

    init = tuple(carry_ref[i * SUBLANES:(i + 1) * SUBLANES, :] for i in range(4))
    final = lax.fori_loop(0, tile, step, init, unroll=8)
    for i in range(4):
        carry_ref[i * SUBLANES:(i + 1) * SUBLANES, :] = final[i]

    ys = []
    for gh in range(2):
        acc = None
        for part in range(2):
            first = part * half_slabs + gh * quarter
            xs = jnp.concatenate(
                [slab_ref[pl.ds((first + s) * pitch, tile), :] for s in range(quarter)], axis=1)
            term = _dot(xs.astype(BF16), wc_ref[part, gh])
            acc = term if part == 0 else acc - term
        ys.append(acc)
    y = jnp.concatenate(ys, axis=1) + d_ref[...] * u
    y = _gelu_tanh(y)
    gate = _dot(y.astype(BF16), gw_ref[...]) + gb_ref[...]
    o_ref[...] = (y * _sigmoid(gate)).astype(o_ref.dtype)


def _s5_mixer(u, lam, wb, wc, d_skip, glu_w, glu_b):
    bsz, seqlen, d_ssm = u.shape
    tile = SCAN_TILE
    n_slab = 2 * lam.shape[1]
    full = lambda a: pl.BlockSpec(a.shape, lambda b, i: (0,) * a.ndim)
    return pl.pallas_call(
        functools.partial(_s5_kernel, n_slab=n_slab),
        grid=(bsz, seqlen // tile),
        in_specs=[
            pl.BlockSpec((pl.Squeezed(), tile, d_ssm), lambda b, i: (b, i, 0)),
            full(lam), full(wb), full(wc), full(d_skip), full(glu_w), full(glu_b),
        ],
        out_specs=pl.BlockSpec((pl.Squeezed(), tile, d_ssm), lambda b, i: (b, i, 0)),
        out_shape=jax.ShapeDtypeStruct((bsz, seqlen, d_ssm), BF16),
        scratch_shapes=[
            pltpu.VMEM((n_slab * SCAN_PITCH, LANES), F32),
            pltpu.VMEM((n_slab, LANES), F32),
        ],
        compiler_params=_params("arbitrary", "arbitrary"),
        name="s5_mixer",
    )(u, lam, wb, wc, d_skip, glu_w, glu_b)


def _attn_kernel(q_ref, k_ref, v_ref, lq1_ref, lk1_ref, lq2_ref, lk2_ref, sub_ref, o_ref,
                 *, lam_init):
    tq = q_ref.shape[0]
    tk = ATT_K_TILE
    qi = pl.program_id(2)
    q = q_ref[...]
    lane = lax.broadcasted_iota(jnp.int32, q.shape, 1)
    zero = jnp.zeros_like(q)
    qq = jnp.concatenate([jnp.where(lane < DIFF_HEAD, q, zero),
                          jnp.where(lane >= DIFF_HEAD, q, zero)], axis=0)

    def block(j, carry, masked):
        m, l, acc = carry
        kj = k_ref[pl.ds(j * tk, tk), :]
        vj = v_ref[pl.ds(j * tk, tk), :]
        s = lax.dot_general(qq, kj, (((1,), (1,)), ((), ())), preferred_element_type=F32)
        if masked:
            row = lax.broadcasted_iota(jnp.int32, s.shape, 0) % tq
            col = lax.broadcasted_iota(jnp.int32, s.shape, 1)
            s = jnp.where(col <= row, s, -jnp.inf)
        m_new = jnp.maximum(m, jnp.max(s, axis=-1, keepdims=True))
        alpha = jnp.exp(m - m_new)
        p = jnp.exp(s - m_new)
        l = alpha * l + jnp.sum(p, axis=-1, keepdims=True)
        acc = alpha * acc + _dot(p.astype(BF16), vj)
        return m_new, l, acc

    init = (jnp.full((2 * tq, 1), -jnp.inf, F32),
            jnp.zeros((2 * tq, 1), F32),
            jnp.zeros((2 * tq, v_ref.shape[1]), F32))
    carry = lax.fori_loop(0, qi, lambda j, c: block(j, c, False), init)
    _, l, acc = block(qi, carry, True)

    lam = (jnp.exp(jnp.sum(lq1_ref[...] * lk1_ref[...], axis=-1, keepdims=True))
           - jnp.exp(jnp.sum(lq2_ref[...] * lk2_ref[...], axis=-1, keepdims=True)) + lam_init)
    o = acc[:tq] / l[:tq] - lam * (acc[tq:] / l[tq:])
    o = o * lax.rsqrt(jnp.mean(o * o, axis=-1, keepdims=True) + RMS_EPS)
    o_ref[...] = (o * (sub_ref[...] * (1.0 - lam_init))).astype(o_ref.dtype)


def _diff_attention(q, k, v, lq1, lk1, lq2, lk2, subln, lam_init):
    bsz, seqlen, d_att = q.shape
    head_w = 2 * DIFF_HEAD
    n_heads = d_att // head_w
    assert ATT_Q_TILE == ATT_K_TILE
    small = lambda a: pl.BlockSpec(a.shape, lambda b, h, i: (0, 0))
    seq = pl.BlockSpec((pl.Squeezed(), seqlen, head_w), lambda b, h, i: (b, 0, h))
    tile = pl.BlockSpec((pl.Squeezed(), ATT_Q_TILE, head_w), lambda b, h, i: (b, i, h))
    return pl.pallas_call(
        functools.partial(_attn_kernel, lam_init=lam_init),
        grid=(bsz, n_heads, seqlen // ATT_Q_TILE),
        in_specs=[tile, seq, seq, small(lq1), small(lk1), small(lq2), small(lk2), small(subln)],
        out_specs=tile,
        out_shape=jax.ShapeDtypeStruct((bsz, seqlen, d_att), BF16),
        compiler_params=_params("arbitrary", "arbitrary", "arbitrary"),
        name="diff_attention",
    )(q, k, v, lq1, lk1, lq2, lk2, subln)


def _out_proj_kernel(x_ref, ssm_ref, att_ref, mod_ref, w_ref, g_ref, b_ref, o_ref, *, alpha, d_ssm):
    gate = mod_ref[2:3, :]
    mix = _dot(ssm_ref[...], w_ref[:d_ssm, :]) + _dot(att_ref[...], w_ref[d_ssm:, :])
    y = alpha * x_ref[...] + (1.0 + gate) * mix
    o_ref[...] = _layer_norm(y, g_ref[...], b_ref[...])


def _out_proj(x, ssm, att, mod, w_out, g, b, alpha):
    bsz, seqlen, d = x.shape
    rows = ROW_TILE
    d_ssm = ssm.shape[-1]
    tok = lambda width: pl.BlockSpec((pl.Squeezed(), rows, width), lambda bb, i: (bb, i, 0))
    full = lambda a: pl.BlockSpec(a.shape, lambda bb, i: (0,) * a.ndim)
    return pl.pallas_call(
        functools.partial(_out_proj_kernel, alpha=alpha, d_ssm=d_ssm),
        grid=(bsz, seqlen // rows),
        in_specs=[
            tok(d), tok(d_ssm), tok(att.shape[-1]),
            pl.BlockSpec((pl.Squeezed(), mod.shape[1], d), lambda bb, i: (bb, 0, 0)),
            full(w_out), full(g), full(b),
        ],
        out_specs=tok(d),
        out_shape=jax.ShapeDtypeStruct(x.shape, F32),
        compiler_params=_params("arbitrary", "arbitrary"),
        name="out_proj_ln",
    )(x, ssm, att, mod, w_out, g, b)


def _ffn_kernel(x_ref, mod_ref, wg_ref, wu_ref, wd_ref, g_ref, b_ref, o_ref, *, alpha):
    shift = mod_ref[3:4, :]
    scale = mod_ref[4:5, :]
    gate = mod_ref[5:6, :]
    x = x_ref[...]
    h = (x * (1.0 + scale) + shift).astype(BF16)
    a = _silu(_dot(h, wg_ref[...])) * _dot(h, wu_ref[...])
    ffn = _dot(a.astype(BF16), wd_ref[...])
    y = alpha * x + (1.0 + gate) * ffn
    o_ref[...] = _layer_norm(y, g_ref[...], b_ref[...])


def _ffn(x, mod, wg, wu, wd, g, b, alpha):
    bsz, seqlen, d = x.shape
    rows = ROW_TILE
    tok = pl.BlockSpec((pl.Squeezed(), rows, d), lambda bb, i: (bb, i, 0))
    resident = lambda a: pl.BlockSpec(a.shape, lambda bb, i: (0,) * a.ndim,
                                      pipeline_mode=pl.Buffered(1))
    full = lambda a: pl.BlockSpec(a.shape, lambda bb, i: (0,) * a.ndim)
    return pl.pallas_call(
        functools.partial(_ffn_kernel, alpha=alpha),
        grid=(bsz, seqlen // rows),
        in_specs=[
            tok,
            pl.BlockSpec((pl.Squeezed(), mod.shape[1], d), lambda bb, i: (bb, 0, 0)),
            resident(wg), resident(wu), resident(wd), full(g), full(b),
        ],
        out_specs=tok,
        out_shape=jax.ShapeDtypeStruct(x.shape, F32),
        compiler_params=_params("arbitrary", "arbitrary"),
        name="swiglu_ln",
    )(x, mod, wg, wu, wd, g, b)


def kernel(x, c, positions, mod_w, mod_b, w_in, ssm_a_re, ssm_a_im, ssm_log_step, ssm_b_re, ssm_b_im, ssm_c_re, ssm_c_im, ssm_d, glu_w, glu_b, lam_q1, lam_k1, lam_q2, lam_k2, subln_w, w_out, ln1_g, ln1_b, ffn_w_gate, ffn_w_up, ffn_w_down, ln2_g, ln2_b):
    bsz, seqlen, d = x.shape
    depth = mod_w.shape[0]
    n_groups = ssm_a_re.shape[1]
    d_ssm = n_groups * SSM_GROUP
    d_att = (w_in.shape[2] - d_ssm) // 3
    alpha = (2 * depth) ** 0.25

    mod = _modulation(c, mod_w, mod_b).reshape(depth, bsz, 6, d)
    ropes = _rope_tables(positions)
    lb_re, lb_im, bb_re, bb_im = _discretise(ssm_a_re, ssm_a_im, ssm_log_step, ssm_b_re, ssm_b_im)
    n_lane_rows = n_groups * SSM_STATE // LANES
    lam = jnp.stack([lb_re.reshape(depth, n_lane_rows, LANES),
                     lb_im.reshape(depth, n_lane_rows, LANES)], axis=1)
    wb = jnp.stack([_block_diag_in(bb_re), _block_diag_in(bb_im)], axis=1).astype(BF16)
    wc = jnp.stack([_block_diag_out(ssm_c_re), _block_diag_out(ssm_c_im)], axis=1).astype(BF16)
    d_skip = ssm_d.reshape(depth, 1, d_ssm)
    row = lambda a: a.reshape(depth, 1, a.shape[-1])

    w_in_b = w_in.astype(BF16)
    glu_w_b = glu_w.astype(BF16)
    w_out_b = w_out.astype(BF16)
    wg_b = ffn_w_gate.astype(BF16)
    wu_b = ffn_w_up.astype(BF16)
    wd_b = ffn_w_down.astype(BF16)

    for l in range(depth):
        lam_init = 0.8 - 0.6 * math.exp(-0.3 * l)
        u, q, k, v = _in_proj(x, mod[l], w_in_b[l], ropes, d_ssm, d_att)
        ssm = _s5_mixer(u, lam[l], wb[l], wc[l], d_skip[l], glu_w_b[l], row(glu_b)[l])
        att = _diff_attention(q, k, v, row(lam_q1)[l], row(lam_k1)[l], row(lam_q2)[l],
                              row(lam_k2)[l], row(subln_w)[l], lam_init)
        x = _out_proj(x, ssm, att, mod[l], w_out_b[l], row(ln1_g)[l], row(ln1_b)[l], alpha)
        x = _ffn(x, mod[l], wg_b[l], wu_b[l], wd_b[l], row(ln2_g)[l], row(ln2_b)[l], alpha)
    return x
```

```python
import functools
import math

import jax
import jax.numpy as jnp
from jax import lax
from jax.experimental import pallas as pl
from jax.experimental.pallas import tpu as pltpu

F32 = jnp.float32
BF16 = jnp.bfloat16

SSM_GROUP = 16
SSM_STATE = 64
DIFF_HEAD = 64
ROT_DIM = DIFF_HEAD // 4
ROPE_THETA = 500000.0
LN_EPS = 1e-5
RMS_EPS = 1e-5

LANES = 128
SUBLANES = 8
BF16_ROWS = 16
MXU_DIM = 256
VMEM_LIMIT_BYTES = 56 * 1024 * 1024

ROW_TILE = 512
SCAN_TILE = 256
SCAN_PITCH = SCAN_TILE + SUBLANES
ATT_Q_TILE = 256
ATT_K_TILE = 256
ATT_HEADS_PER_STEP = 4


def _sigmoid(x):
    return 1.0 / (1.0 + jnp.exp(-x))


def _silu(x):
    return x * _sigmoid(x)


def _gelu_tanh(x):
    c = math.sqrt(2.0 / math.pi)
    return 0.5 * x * (1.0 + jnp.tanh(c * (x + 0.044715 * (x * x * x))))


def _dot(a, b):
    return jnp.dot(a, b, preferred_element_type=F32)


def _layer_norm(y, g, b):
    mu = jnp.mean(y, axis=-1, keepdims=True)
    yc = y - mu
    var = jnp.mean(yc * yc, axis=-1, keepdims=True)
    return yc * lax.rsqrt(var + LN_EPS) * g + b


def _params(*sem):
    return pltpu.CompilerParams(dimension_semantics=sem, vmem_limit_bytes=VMEM_LIMIT_BYTES)


def _mod_kernel(c_ref, w_ref, b_ref, o_ref):
    cond = _silu(c_ref[...])
    o_ref[...] = _dot(cond.astype(BF16), w_ref[...].astype(BF16)) + b_ref[...]


def _modulation(c, mod_w, mod_b):
    depth, d, six_d = mod_w.shape
    bsz = c.shape[0]
    n_col = six_d // d
    return pl.pallas_call(
        _mod_kernel,
        grid=(depth, n_col),
        in_specs=[
            pl.BlockSpec((bsz, d), lambda l, j: (0, 0)),
            pl.BlockSpec((pl.Squeezed(), d, d), lambda l, j: (l, 0, j)),
            pl.BlockSpec((pl.Squeezed(), 1, d), lambda l, j: (l, 0, j)),
        ],
        out_specs=pl.BlockSpec((pl.Squeezed(), bsz, d), lambda l, j: (l, 0, j)),
        out_shape=jax.ShapeDtypeStruct((depth, bsz, six_d), F32),
        compiler_params=_params("arbitrary", "arbitrary"),
        name="adaln_modulation",
    )(c, mod_w, mod_b.reshape(depth, 1, six_d))


def _rope_kernel(pos_ref, freq_ref, cos_ref, sina_ref, sinb_ref):
    pos = pos_ref[...].astype(F32)
    ang = pos * freq_ref[...]
    lane = lax.broadcasted_iota(jnp.int32, ang.shape, 1) % DIFF_HEAD
    c = jnp.cos(ang)
    s = jnp.sin(ang)
    half = ROT_DIM // 2
    cos_ref[...] = jnp.where(lane < ROT_DIM, c, 1.0)
    sina_ref[...] = jnp.where(lane < half, -s, 0.0)
    sinb_ref[...] = jnp.where((lane >= half) & (lane < ROT_DIM), s, 0.0)


def _rope_tables(positions):
    bsz, seqlen = positions.shape
    half = ROT_DIM // 2
    freqs = ROPE_THETA ** (-jnp.arange(0, ROT_DIM, 2, dtype=F32) / ROT_DIM)
    lane = jnp.arange(LANES) % DIFF_HEAD
    freq_lane = jnp.where(lane < ROT_DIM, freqs[lane % half], 0.0).reshape(1, LANES)
    rows = ROW_TILE
    spec = pl.BlockSpec((pl.Squeezed(), rows, LANES), lambda b, i: (b, i, 0))
    shape = jax.ShapeDtypeStruct((bsz, seqlen, LANES), F32)
    return pl.pallas_call(
        _rope_kernel,
        grid=(bsz, seqlen // rows),
        in_specs=[
            pl.BlockSpec((pl.Squeezed(), rows, 1), lambda b, i: (b, i, 0)),
            pl.BlockSpec((1, LANES), lambda b, i: (0, 0)),
        ],
        out_specs=[spec, spec, spec],
        out_shape=[shape, shape, shape],
        compiler_params=_params("arbitrary", "arbitrary"),
        name="rope_tables",
    )(positions.reshape(bsz, seqlen, 1), freq_lane)


def _disc_kernel(are_ref, aim_ref, ls_ref, bre_ref, bim_ref,
                 lbre_ref, lbim_ref, bbre_ref, bbim_ref):
    delta = jnp.exp(ls_ref[...])
    lam_re = jnp.minimum(are_ref[...], -1e-4)
    lam_im = aim_ref[...]
    mag = jnp.exp(lam_re * delta)
    ang = lam_im * delta
    lb_re = mag * jnp.cos(ang)
    lb_im = mag * jnp.sin(ang)
    den = lam_re * lam_re + lam_im * lam_im
    n_re = lb_re - 1.0
    n_im = lb_im
    k_re = (n_re * lam_re + n_im * lam_im) / den
    k_im = (n_im * lam_re - n_re * lam_im) / den
    b_re = bre_ref[...]
    b_im = bim_ref[...]
    lbre_ref[...] = lb_re
    lbim_ref[...] = lb_im
    bbre_ref[...] = k_re * b_re - k_im * b_im
    bbim_ref[...] = k_re * b_im + k_im * b_re


def _discretise(a_re, a_im, log_step, b_re, b_im):
    depth, g, p = a_re.shape
    c = b_re.shape[-1]
    rows = depth * g * c

    def rep(a):
        return jnp.broadcast_to(a[:, :, None, :], (depth, g, c, p)).reshape(rows, p)

    ls = jnp.broadcast_to(log_step[:, :, None, None], (depth, g, c, 1)).reshape(rows, 1)
    bt_re = b_re.transpose(0, 1, 3, 2).reshape(rows, p)
    bt_im = b_im.transpose(0, 1, 3, 2).reshape(rows, p)
    shape = jax.ShapeDtypeStruct((rows, p), F32)
    outs = pl.pallas_call(
        _disc_kernel,
        out_shape=[shape, shape, shape, shape],
        name="s5_discretise",
    )(rep(a_re), rep(a_im), ls, bt_re, bt_im)
    lb_re, lb_im, bb_re, bb_im = [o.reshape(depth, g, c, p) for o in outs]
    return lb_re[:, :, 0, :], lb_im[:, :, 0, :], bb_re, bb_im


def _block_diag_in(w):
    depth, g, c, p = w.shape
    gh = g // 2
    eye = jnp.eye(gh, dtype=w.dtype)
    w = w.reshape(depth, 2, gh, c, p)
    return jnp.einsum("lhgcp,gk->lhgckp", w, eye).reshape(depth, 2, gh * c, gh * p)


def _block_diag_out(w):
    depth, g, c, p = w.shape
    gh = g // 2
    eye = jnp.eye(gh, dtype=w.dtype)
    w = w.reshape(depth, 2, gh, c, p)
    return jnp.einsum("lhgcp,gk->lhgpkc", w, eye).reshape(depth, 2, gh * p, gh * c)


def _in_proj_kernel(x_ref, mod_ref, w_ref, cos_ref, sina_ref, sinb_ref,
                    u_ref, qt_ref, k_ref, vt_ref, *, d_ssm, d_att):
    shift = mod_ref[0:1, :]
    scale = mod_ref[1:2, :]
    h = x_ref[...] * (1.0 + scale) + shift
    proj = _dot(h.astype(BF16), w_ref[...])
    u_ref[...] = proj[:, :d_ssm]
    cos_t = cos_ref[...]
    sin_a = sina_ref[...]
    sin_b = sinb_ref[...]
    half = ROT_DIM // 2

    def rotary(t):
        return t * cos_t + pltpu.roll(t, LANES - half, 1) * sin_a + pltpu.roll(t, half, 1) * sin_b

    q_scale = DIFF_HEAD ** -0.5 * math.log2(math.e)
    for blk in range(d_att // LANES):
        cols = slice(blk * LANES, (blk + 1) * LANES)
        lo = d_ssm + blk * LANES
        qt_ref[cols, :] = (rotary(proj[:, lo:lo + LANES]) * q_scale).T.astype(BF16)
        lo = d_ssm + d_att + blk * LANES
        k_ref[:, cols] = rotary(proj[:, lo:lo + LANES]).astype(BF16)
        lo = d_ssm + 2 * d_att + blk * LANES
        vt = proj[:, lo:lo + LANES].T.astype(BF16)
        for kb in range(vt_ref.shape[0]):
            vt_ref[kb, cols, :] = vt[:, kb * ATT_K_TILE:(kb + 1) * ATT_K_TILE]


def _in_proj(x, mod, w_in, ropes, d_ssm, d_att):
    bsz, seqlen, d = x.shape
    rows = ROW_TILE
    n_cols = w_in.shape[1]
    kt = rows // ATT_K_TILE
    tok = lambda width: pl.BlockSpec((pl.Squeezed(), rows, width), lambda b, i: (b, i, 0))
    return pl.pallas_call(
        functools.partial(_in_proj_kernel, d_ssm=d_ssm, d_att=d_att),
        grid=(bsz, seqlen // rows),
        in_specs=[
            tok(d),
            pl.BlockSpec((pl.Squeezed(), mod.shape[1], d), lambda b, i: (b, 0, 0)),
            pl.BlockSpec((d, n_cols), lambda b, i: (0, 0)),
            tok(LANES), tok(LANES), tok(LANES),
        ],
        out_specs=[
            tok(d_ssm),
            pl.BlockSpec((pl.Squeezed(), d_att, rows), lambda b, i: (b, 0, i)),
            tok(d_att),
            pl.BlockSpec((pl.Squeezed(), kt, d_att, ATT_K_TILE), lambda b, i: (b, i, 0, 0)),
        ],
        out_shape=[
            jax.ShapeDtypeStruct((bsz, seqlen, d_ssm), F32),
            jax.ShapeDtypeStruct((bsz, d_att, seqlen), BF16),
            jax.ShapeDtypeStruct((bsz, seqlen, d_att), BF16),
            jax.ShapeDtypeStruct((bsz, seqlen // ATT_K_TILE, d_att, ATT_K_TILE), BF16),
        ],
        compiler_params=_params("arbitrary", "arbitrary"),
        name="in_proj_rotary",
    )(x, mod, w_in, *ropes)


def _s5_kernel(u_ref, lam_ref, wb_ref, wc_ref, d_ref, gw_ref, gb_ref, o_ref,
               slab_ref, carry_ref, *, n_slab):
    tile = u_ref.shape[0]
    pitch = SCAN_PITCH
    half_slabs = n_slab // 2
    quarter = half_slabs // 2

    @pl.when(pl.program_id(1) == 0)
    def _():
        carry_ref[...] = jnp.zeros_like(carry_ref)

    u = u_ref[...]
    ub = u.astype(BF16)
    half_w = ub.shape[1] // 2
    for part in range(2):
        for gh in range(2):
            r = _dot(ub[:, gh * half_w:(gh + 1) * half_w], wb_ref[part, gh])
            for s in range(quarter):
                slab = part * half_slabs + gh * quarter + s
                slab_ref[pl.ds(slab * pitch, tile), :] = r[:, s * LANES:(s + 1) * LANES]

    lam_re = [lam_ref[0, v * SUBLANES:(v + 1) * SUBLANES, :] for v in range(2)]
    lam_im = [lam_ref[1, v * SUBLANES:(v + 1) * SUBLANES, :] for v in range(2)]

    def rows(t, part, v):
        first = (part * half_slabs + v * SUBLANES) * pitch
        return pl.ds(first + t, SUBLANES, stride=pitch)

    def step(t, state):
        new = []
        for v in range(2):
            x_re, x_im = state[2 * v], state[2 * v + 1]
            b_re = slab_ref[rows(t, 0, v), :]
            b_im = slab_ref[rows(t, 1, v), :]
            n_re = lam_re[v] * x_re - lam_im[v] * x_im + b_re
            n_im = lam_re[v] * x_im + lam_im[v] * x_re + b_im
            slab_ref[rows(t, 0, v), :] = n_re
            slab_ref[rows(t, 1, v), :] = n_im
            new += [n_re, n_im]
        return tuple(new)

    init = tuple(carry_ref[i * SUBLANES:(i + 1) * SUBLANES, :] for i in range(4))
    final = lax.fori_loop(0, tile, step, init, unroll=8)
    for i in range(4):
        carry_ref[i * SUBLANES:(i + 1) * SUBLANES, :] = final[i]

    ys = []
    for gh in range(2):
        acc = None
        for part in range(2):
            first = part * half_slabs + gh * quarter
            xs = jnp.concatenate(
                [slab_ref[pl.ds((first + s) * pitch, tile), :] for s in range(quarter)], axis=1)
            term = _dot(xs.astype(BF16), wc_ref[part, gh])
            acc = term if part == 0 else acc - term
        ys.append(acc)
    y = jnp.concatenate(ys, axis=1) + d_ref[...] * u
    y = _gelu_tanh(y)
    gate = _dot(y.astype(BF16), gw_ref[...]) + gb_ref[...]
    o_ref[...] = (y * _sigmoid(gate)).astype(o_ref.dtype)


def _s5_mixer(u, lam, wb, wc, d_skip, glu_w, glu_b):
    bsz, seqlen, d_ssm = u.shape
    tile = SCAN_TILE
    n_slab = 2 * lam.shape[1]
    full = lambda a: pl.BlockSpec(a.shape, lambda b, i: (0,) * a.ndim)
    return pl.pallas_call(
        functools.partial(_s5_kernel, n_slab=n_slab),
        grid=(bsz, seqlen // tile),
        in_specs=[
            pl.BlockSpec((pl.Squeezed(), tile, d_ssm), lambda b, i: (b, i, 0)),
            full(lam), full(wb), full(wc), full(d_skip), full(glu_w), full(glu_b),
        ],
        out_specs=pl.BlockSpec((pl.Squeezed(), tile, d_ssm), lambda b, i: (b, i, 0)),
        out_shape=jax.ShapeDtypeStruct((bsz, seqlen, d_ssm), BF16),
        scratch_shapes=[
            pltpu.VMEM((n_slab * SCAN_PITCH, LANES), F32),
            pltpu.VMEM((n_slab, LANES), F32),
        ],
        compiler_params=_params("arbitrary", "arbitrary"),
        name="s5_mixer",
    )(u, lam, wb, wc, d_skip, glu_w, glu_b)


def _attn_kernel(qt_ref, k_ref, vt_ref, lq1_ref, lk1_ref, lq2_ref, lk2_ref, sub_ref, o_ref,
                 p_ref, acc_ref, *, lam_init, heads):
    tq = qt_ref.shape[1]
    tk = ATT_K_TILE
    head_w = 2 * DIFF_HEAD
    qi = pl.program_id(2)
    ones = jnp.ones((BF16_ROWS, tk), BF16)

    rhs = []
    for h in range(heads):
        qt = qt_ref[h * head_w:(h + 1) * head_w, :]
        sub = lax.broadcasted_iota(jnp.int32, qt.shape, 0)
        zero = jnp.zeros_like(qt)
        rhs.append(jnp.concatenate([jnp.where(sub < DIFF_HEAD, qt, zero),
                                    jnp.where(sub >= DIFF_HEAD, qt, zero)], axis=1))

    def scores(j, h):
        start = pl.multiple_of(j * tk, tk)
        return _dot(k_ref[pl.ds(start, tk), h * head_w:(h + 1) * head_w], rhs[h])

    def pv(j, h):
        lhs = jnp.concatenate([vt_ref[j, h * head_w:(h + 1) * head_w, :], ones], axis=0)
        return _dot(lhs, p_ref[h])

    stats = []
    for h in range(heads):
        s = scores(qi, h)
        key = lax.broadcasted_iota(jnp.int32, s.shape, 0)
        qry = lax.broadcasted_iota(jnp.int32, s.shape, 1) % tq
        s = jnp.where(key <= qry, s, -jnp.inf)
        m = jnp.max(s, axis=0, keepdims=True)
        p_ref[h] = jnp.exp2(s - m).astype(BF16)
        acc_ref[h] = jnp.zeros(acc_ref.shape[1:], F32)
        stats.append((m, jnp.ones_like(m)))

    def step(j, pending, carry):
        out = []
        for h in range(heads):
            m, alpha_p = carry[h]
            acc_ref[h] = alpha_p * acc_ref[h] + pv(pending, h)
            s = scores(j, h)
            m_new = jnp.maximum(m, jnp.max(s, axis=0, keepdims=True))
            p_ref[h] = jnp.exp2(s - m_new).astype(BF16)
            out.append((m_new, jnp.exp2(m - m_new)))
        return tuple(out)

    def pair(i, carry):
        j = 2 * i
        carry = step(j, jnp.where(i == 0, qi, j - 1), carry)
        return step(j + 1, j, carry)

    carry = lax.fori_loop(0, qi // 2, pair, tuple(stats))
    last = qi - 1
    carry = lax.fori_loop(
        0, qi % 2, lambda _, c: step(last, jnp.where(last == 0, qi, last - 1), c), carry)
    pending = jnp.where(qi == 0, qi, last)

    lam = (jnp.exp(jnp.sum(lq1_ref[...] * lk1_ref[...], axis=-1, keepdims=True))
           - jnp.exp(jnp.sum(lq2_ref[...] * lk2_ref[...], axis=-1, keepdims=True)) + lam_init)
    gain = sub_ref[...] * (1.0 - lam_init)
    for h in range(heads):
        _, alpha_p = carry[h]
        acc = alpha_p * acc_ref[h] + pv(pending, h)
        inv = 1.0 / acc[head_w:head_w + 1, :]
        o = acc[:head_w, :tq] * inv[:, :tq] - acc[:head_w, tq:] * (lam * inv[:, tq:])
        o = o * lax.rsqrt(jnp.mean(o * o, axis=0, keepdims=True) + RMS_EPS) * gain
        o_ref[:, h * head_w:(h + 1) * head_w] = o.T.astype(o_ref.dtype)


def _diff_attention(qt, k, vt, lq1, lk1, lq2, lk2, subln_col, lam_init):
    bsz, seqlen, d_att = k.shape
    head_w = 2 * DIFF_HEAD
    heads = ATT_HEADS_PER_STEP
    width = heads * head_w
    assert ATT_Q_TILE == ATT_K_TILE and d_att % width == 0
    small = lambda a: pl.BlockSpec(a.shape, lambda b, h, i: (0, 0))
    return pl.pallas_call(
        functools.partial(_attn_kernel, lam_init=lam_init, heads=heads),
        grid=(bsz, d_att // width, seqlen // ATT_Q_TILE),
        in_specs=[
            pl.BlockSpec((pl.Squeezed(), width, ATT_Q_TILE), lambda b, h, i: (b, h, i)),
            pl.BlockSpec((pl.Squeezed(), seqlen, width), lambda b, h, i: (b, 0, h)),
            pl.BlockSpec((pl.Squeezed(), seqlen // ATT_K_TILE, width, ATT_K_TILE),
                         lambda b, h, i: (b, 0, h, 0)),
            small(lq1), small(lk1), small(lq2), small(lk2), small(subln_col),
        ],
        out_specs=pl.BlockSpec((pl.Squeezed(), ATT_Q_TILE, width), lambda b, h, i: (b, i, h)),
        out_shape=jax.ShapeDtypeStruct((bsz, seqlen, d_att), BF16),
        scratch_shapes=[
            pltpu.VMEM((heads, ATT_K_TILE, 2 * ATT_Q_TILE), BF16),
            pltpu.VMEM((heads, head_w + BF16_ROWS, 2 * ATT_Q_TILE), F32),
        ],
        compiler_params=_params("arbitrary", "arbitrary", "arbitrary"),
        name="diff_attention",
    )(qt, k, vt, lq1, lk1, lq2, lk2, subln_col)


def _out_proj_kernel(x_ref, ssm_ref, att_ref, mod_ref, w_ref, g_ref, b_ref, o_ref, *, alpha, d_ssm):
    gate = mod_ref[2:3, :]
    mix = _dot(ssm_ref[...], w_ref[:d_ssm, :]) + _dot(att_ref[...], w_ref[d_ssm:, :])
    y = alpha * x_ref[...] + (1.0 + gate) * mix
    o_ref[...] = _layer_norm(y, g_ref[...], b_ref[...])


def _out_proj(x, ssm, att, mod, w_out, g, b, alpha):
    bsz, seqlen, d = x.shape
    rows = ROW_TILE
    d_ssm = ssm.shape[-1]
    tok = lambda width: pl.BlockSpec((pl.Squeezed(), rows, width), lambda bb, i: (bb, i, 0))
    full = lambda a: pl.BlockSpec(a.shape, lambda bb, i: (0,) * a.ndim)
    return pl.pallas_call(
        functools.partial(_out_proj_kernel, alpha=alpha, d_ssm=d_ssm),
        grid=(bsz, seqlen // rows),
        in_specs=[
            tok(d), tok(d_ssm), tok(att.shape[-1]),
            pl.BlockSpec((pl.Squeezed(), mod.shape[1], d), lambda bb, i: (bb, 0, 0)),
            full(w_out), full(g), full(b),
        ],
        out_specs=tok(d),
        out_shape=jax.ShapeDtypeStruct(x.shape, F32),
        compiler_params=_params("arbitrary", "arbitrary"),
        name="out_proj_ln",
    )(x, ssm, att, mod, w_out, g, b)


def _ffn_kernel(x_ref, mod_ref, wg_ref, wu_ref, wd_ref, g_ref, b_ref, o_ref, *, alpha):
    shift = mod_ref[3:4, :]
    scale = mod_ref[4:5, :]
    gate = mod_ref[5:6, :]
    x = x_ref[...]
    h = (x * (1.0 + scale) + shift).astype(BF16)
    a = _silu(_dot(h, wg_ref[...])) * _dot(h, wu_ref[...])
    ffn = _dot(a.astype(BF16), wd_ref[...])
    y = alpha * x + (1.0 + gate) * ffn
    o_ref[...] = _layer_norm(y, g_ref[...], b_ref[...])


def _ffn(x, mod, wg, wu, wd, g, b, alpha):
    bsz, seqlen, d = x.shape
    rows = ROW_TILE
    tok = pl.BlockSpec((pl.Squeezed(), rows, d), lambda bb, i: (bb, i, 0))
    resident = lambda a: pl.BlockSpec(a.shape, lambda bb, i: (0,) * a.ndim,
                                      pipeline_mode=pl.Buffered(1))
    full = lambda a: pl.BlockSpec(a.shape, lambda bb, i: (0,) * a.ndim)
    return pl.pallas_call(
        functools.partial(_ffn_kernel, alpha=alpha),
        grid=(bsz, seqlen // rows),
        in_specs=[
            tok,
            pl.BlockSpec((pl.Squeezed(), mod.shape[1], d), lambda bb, i: (bb, 0, 0)),
            resident(wg), resident(wu), resident(wd), full(g), full(b),
        ],
        out_specs=tok,
        out_shape=jax.ShapeDtypeStruct(x.shape, F32),
        compiler_params=_params("arbitrary", "arbitrary"),
        name="swiglu_ln",
    )(x, mod, wg, wu, wd, g, b)


def kernel(x, c, positions, mod_w, mod_b, w_in, ssm_a_re, ssm_a_im, ssm_log_step, ssm_b_re, ssm_b_im, ssm_c_re, ssm_c_im, ssm_d, glu_w, glu_b, lam_q1, lam_k1, lam_q2, lam_k2, subln_w, w_out, ln1_g, ln1_b, ffn_w_gate, ffn_w_up, ffn_w_down, ln2_g, ln2_b):
    bsz, seqlen, d = x.shape
    depth = mod_w.shape[0]
    n_groups = ssm_a_re.shape[1]
    d_ssm = n_groups * SSM_GROUP
    d_att = (w_in.shape[2] - d_ssm) // 3
    alpha = (2 * depth) ** 0.25

    mod = _modulation(c, mod_w, mod_b).reshape(depth, bsz, 6, d)
    ropes = _rope_tables(positions)
    lb_re, lb_im, bb_re, bb_im = _discretise(ssm_a_re, ssm_a_im, ssm_log_step, ssm_b_re, ssm_b_im)
    n_lane_rows = n_groups * SSM_STATE // LANES
    lam = jnp.stack([lb_re.reshape(depth, n_lane_rows, LANES),
                     lb_im.reshape(depth, n_lane_rows, LANES)], axis=1)
    wb = jnp.stack([_block_diag_in(bb_re), _block_diag_in(bb_im)], axis=1).astype(BF16)
    wc = jnp.stack([_block_diag_out(ssm_c_re), _block_diag_out(ssm_c_im)], axis=1).astype(BF16)
    d_skip = ssm_d.reshape(depth, 1, d_ssm)
    row = lambda a: a.reshape(depth, 1, a.shape[-1])

    w_in_b = w_in.astype(BF16)
    glu_w_b = glu_w.astype(BF16)
    w_out_b = w_out.astype(BF16)
    wg_b = ffn_w_gate.astype(BF16)
    wu_b = ffn_w_up.astype(BF16)
    wd_b = ffn_w_down.astype(BF16)

    for l in range(depth):
        lam_init = 0.8 - 0.6 * math.exp(-0.3 * l)
        u, qt, k, vt = _in_proj(x, mod[l], w_in_b[l], ropes, d_ssm, d_att)
        ssm = _s5_mixer(u, lam[l], wb[l], wc[l], d_skip[l], glu_w_b[l], row(glu_b)[l])
        att = _diff_attention(qt, k, vt, row(lam_q1)[l], row(lam_k1)[l], row(lam_q2)[l],
                              row(lam_k2)[l], subln_w[l].reshape(-1, 1), lam_init)
        x = _out_proj(x, ssm, att, mod[l], w_out_b[l], row(ln1_g)[l], row(ln1_b)[l], alpha)
        x = _ffn(x, mod[l], wg_b[l], wu_b[l], wd_b[l], row(ln2_g)[l], row(ln2_b)[l], alpha)
    return x
```

```python
import functools
import math

import jax
import jax.numpy as jnp
from jax import lax
from jax.experimental import pallas as pl
from jax.experimental.pallas import tpu as pltpu

F32 = jnp.float32
BF16 = jnp.bfloat16

SSM_GROUP = 16
SSM_STATE = 64
DIFF_HEAD = 64
ROT_DIM = DIFF_HEAD // 4
ROPE_THETA = 500000.0
LN_EPS = 1e-5
RMS_EPS = 1e-5

LANES = 128
SUBLANES = 8
BF16_ROWS = 16
MXU_DIM = 256
VMEM_LIMIT_BYTES = 56 * 1024 * 1024

ROW_TILE = 512
SCAN_TILE = 256
SCAN_PAD = SUBLANES
SCAN_PITCH = SCAN_TILE + 2 * SCAN_PAD
SCAN_BATCH = 2
ATT_Q_TILE = 256
ATT_K_TILE = 256
ATT_HEADS_PER_STEP = 4


def _sigmoid(x):
    return 1.0 / (1.0 + jnp.exp(-x))


def _silu(x):
    return x * _sigmoid(x)


def _gelu_tanh(x):
    c = math.sqrt(2.0 / math.pi)
    return 0.5 * x * (1.0 + jnp.tanh(c * (x + 0.044715 * (x * x * x))))


def _dot(a, b):
    return jnp.dot(a, b, preferred_element_type=F32)


def _layer_norm(y, g, b):
    mu = jnp.mean(y, axis=-1, keepdims=True)
    yc = y - mu
    var = jnp.mean(yc * yc, axis=-1, keepdims=True)
    return yc * lax.rsqrt(var + LN_EPS) * g + b


def _params(*sem):
    return pltpu.CompilerParams(dimension_semantics=sem, vmem_limit_bytes=VMEM_LIMIT_BYTES)


def _mod_kernel(c_ref, w_ref, b_ref, o_ref):
    cond = _silu(c_ref[...])
    o_ref[...] = _dot(cond.astype(BF16), w_ref[...].astype(BF16)) + b_ref[...]


def _modulation(c, mod_w, mod_b):
    depth, d, six_d = mod_w.shape
    bsz = c.shape[0]
    n_col = six_d // d
    return pl.pallas_call(
        _mod_kernel,
        grid=(depth, n_col),
        in_specs=[
            pl.BlockSpec((bsz, d), lambda l, j: (0, 0)),
            pl.BlockSpec((pl.Squeezed(), d, d), lambda l, j: (l, 0, j)),
            pl.BlockSpec((pl.Squeezed(), 1, d), lambda l, j: (l, 0, j)),
        ],
        out_specs=pl.BlockSpec((pl.Squeezed(), bsz, d), lambda l, j: (l, 0, j)),
        out_shape=jax.ShapeDtypeStruct((depth, bsz, six_d), F32),
        compiler_params=_params("arbitrary", "arbitrary"),
        name="adaln_modulation",
    )(c, mod_w, mod_b.reshape(depth, 1, six_d))


def _rope_kernel(pos_ref, freq_ref, cos_ref, sina_ref, sinb_ref):
    pos = pos_ref[...].astype(F32)
    ang = pos * freq_ref[...]
    lane = lax.broadcasted_iota(jnp.int32, ang.shape, 1) % DIFF_HEAD
    c = jnp.cos(ang)
    s = jnp.sin(ang)
    half = ROT_DIM // 2
    cos_ref[...] = jnp.where(lane < ROT_DIM, c, 1.0)
    sina_ref[...] = jnp.where(lane < half, -s, 0.0)
    sinb_ref[...] = jnp.where((lane >= half) & (lane < ROT_DIM), s, 0.0)


def _rope_tables(positions):
    bsz, seqlen = positions.shape
    half = ROT_DIM // 2
    freqs = ROPE_THETA ** (-jnp.arange(0, ROT_DIM, 2, dtype=F32) / ROT_DIM)
    lane = jnp.arange(LANES) % DIFF_HEAD
    freq_lane = jnp.where(lane < ROT_DIM, freqs[lane % half], 0.0).reshape(1, LANES)
    rows = ROW_TILE
    spec = pl.BlockSpec((pl.Squeezed(), rows, LANES), lambda b, i: (b, i, 0))
    shape = jax.ShapeDtypeStruct((bsz, seqlen, LANES), F32)
    return pl.pallas_call(
        _rope_kernel,
        grid=(bsz, seqlen // rows),
        in_specs=[
            pl.BlockSpec((pl.Squeezed(), rows, 1), lambda b, i: (b, i, 0)),
            pl.BlockSpec((1, LANES), lambda b, i: (0, 0)),
        ],
        out_specs=[spec, spec, spec],
        out_shape=[shape, shape, shape],
        compiler_params=_params("arbitrary", "arbitrary"),
        name="rope_tables",
    )(positions.reshape(bsz, seqlen, 1), freq_lane)


def _disc_kernel(are_ref, aim_ref, ls_ref, bre_ref, bim_ref,
                 lbre_ref, lbim_ref, bbre_ref, bbim_ref):
    delta = jnp.exp(ls_ref[...])
    lam_re = jnp.minimum(are_ref[...], -1e-4)
    lam_im = aim_ref[...]
    mag = jnp.exp(lam_re * delta)
    ang = lam_im * delta
    lb_re = mag * jnp.cos(ang)
    lb_im = mag * jnp.sin(ang)
    den = lam_re * lam_re + lam_im * lam_im
    n_re = lb_re - 1.0
    n_im = lb_im
    k_re = (n_re * lam_re + n_im * lam_im) / den
    k_im = (n_im * lam_re - n_re * lam_im) / den
    b_re = bre_ref[...]
    b_im = bim_ref[...]
    lbre_ref[...] = lb_re
    lbim_ref[...] = lb_im
    bbre_ref[...] = k_re * b_re - k_im * b_im
    bbim_ref[...] = k_re * b_im + k_im * b_re


def _discretise(a_re, a_im, log_step, b_re, b_im):
    depth, g, p = a_re.shape
    c = b_re.shape[-1]
    rows = depth * g * c

    def rep(a):
        return jnp.broadcast_to(a[:, :, None, :], (depth, g, c, p)).reshape(rows, p)

    ls = jnp.broadcast_to(log_step[:, :, None, None], (depth, g, c, 1)).reshape(rows, 1)
    bt_re = b_re.transpose(0, 1, 3, 2).reshape(rows, p)
    bt_im = b_im.transpose(0, 1, 3, 2).reshape(rows, p)
    shape = jax.ShapeDtypeStruct((rows, p), F32)
    outs = pl.pallas_call(
        _disc_kernel,
        out_shape=[shape, shape, shape, shape],
        name="s5_discretise",
    )(rep(a_re), rep(a_im), ls, bt_re, bt_im)
    lb_re, lb_im, bb_re, bb_im = [o.reshape(depth, g, c, p) for o in outs]
    return lb_re[:, :, 0, :], lb_im[:, :, 0, :], bb_re, bb_im


def _block_diag_in(w):
    depth, g, c, p = w.shape
    gh = g // 2
    eye = jnp.eye(gh, dtype=w.dtype)
    w = w.reshape(depth, 2, gh, c, p)
    return jnp.einsum("lhgcp,gk->lhgckp", w, eye).reshape(depth, 2, gh * c, gh * p)


def _block_diag_out(w):
    depth, g, c, p = w.shape
    gh = g // 2
    eye = jnp.eye(gh, dtype=w.dtype)
    w = w.reshape(depth, 2, gh, c, p)
    return jnp.einsum("lhgcp,gk->lhgpkc", w, eye).reshape(depth, 2, gh * p, gh * c)


def _in_proj_kernel(x_ref, mod_ref, w_ref, cos_ref, sina_ref, sinb_ref,
                    u_ref, qt_ref, k_ref, vt_ref, *, d_ssm, d_att):
    shift = mod_ref[0:1, :]
    scale = mod_ref[1:2, :]
    h = x_ref[...] * (1.0 + scale) + shift
    proj = _dot(h.astype(BF16), w_ref[...])
    u_ref[...] = proj[:, :d_ssm]
    cos_t = cos_ref[...]
    sin_a = sina_ref[...]
    sin_b = sinb_ref[...]
    half = ROT_DIM // 2

    def rotary(t):
        return t * cos_t + pltpu.roll(t, LANES - half, 1) * sin_a + pltpu.roll(t, half, 1) * sin_b

    q_scale = DIFF_HEAD ** -0.5 * math.log2(math.e)
    for blk in range(d_att // LANES):
        cols = slice(blk * LANES, (blk + 1) * LANES)
        lo = d_ssm + blk * LANES
        qt_ref[cols, :] = (rotary(proj[:, lo:lo + LANES]) * q_scale).T.astype(BF16)
        lo = d_ssm + d_att + blk * LANES
        k_ref[:, cols] = rotary(proj[:, lo:lo + LANES]).astype(BF16)
        lo = d_ssm + 2 * d_att + blk * LANES
        vt = proj[:, lo:lo + LANES].T.astype(BF16)
        for kb in range(vt_ref.shape[0]):
            vt_ref[kb, cols, :] = vt[:, kb * ATT_K_TILE:(kb + 1) * ATT_K_TILE]


def _in_proj(x, mod, w_in, ropes, d_ssm, d_att):
    bsz, seqlen, d = x.shape
    rows = ROW_TILE
    n_cols = w_in.shape[1]
    kt = rows // ATT_K_TILE
    tok = lambda width: pl.BlockSpec((pl.Squeezed(), rows, width), lambda b, i: (b, i, 0))
    return pl.pallas_call(
        functools.partial(_in_proj_kernel, d_ssm=d_ssm, d_att=d_att),
        grid=(bsz, seqlen // rows),
        in_specs=[
            tok(d),
            pl.BlockSpec((pl.Squeezed(), mod.shape[1], d), lambda b, i: (b, 0, 0)),
            pl.BlockSpec((d, n_cols), lambda b, i: (0, 0)),
            tok(LANES), tok(LANES), tok(LANES),
        ],
        out_specs=[
            tok(d_ssm),
            pl.BlockSpec((pl.Squeezed(), d_att, rows), lambda b, i: (b, 0, i)),
            tok(d_att),
            pl.BlockSpec((pl.Squeezed(), kt, d_att, ATT_K_TILE), lambda b, i: (b, i, 0, 0)),
        ],
        out_shape=[
            jax.ShapeDtypeStruct((bsz, seqlen, d_ssm), F32),
            jax.ShapeDtypeStruct((bsz, d_att, seqlen), BF16),
            jax.ShapeDtypeStruct((bsz, seqlen, d_att), BF16),
            jax.ShapeDtypeStruct((bsz, seqlen // ATT_K_TILE, d_att, ATT_K_TILE), BF16),
        ],
        compiler_params=_params("arbitrary", "arbitrary"),
        name="in_proj_rotary",
    )(x, mod, w_in, *ropes)


def _s5_kernel(u_ref, lam_ref, wb_ref, wc_ref, d_ref, gw_ref, gb_ref, o_ref,
               *scratch, n_slab, n_seq):
    slab_refs = scratch[:n_seq]
    carry_ref = scratch[n_seq]
    tile = u_ref.shape[1]
    pitch = SCAN_PITCH
    half_slabs = n_slab // 2
    quarter = half_slabs // 2
    skew = SUBLANES - 1

    @pl.when(pl.program_id(1) == 0)
    def _():
        carry_ref[...] = jnp.zeros_like(carry_ref)

    half_w = u_ref.shape[2] // 2
    for b in range(n_seq):
        ub = u_ref[b].astype(BF16)
        pad = jnp.zeros((SCAN_PAD, LANES), F32)
        for part in range(2):
            for gh in range(2):
                r = _dot(ub[:, gh * half_w:(gh + 1) * half_w], wb_ref[part, gh])
                for s in range(quarter):
                    base = (part * half_slabs + gh * quarter + s) * pitch
                    slab_refs[b][pl.ds(base, SCAN_PAD), :] = pad
                    slab_refs[b][pl.ds(base + SCAN_PAD, tile), :] = r[:, s * LANES:(s + 1) * LANES]
                    slab_refs[b][pl.ds(base + SCAN_PAD + tile, SCAN_PAD), :] = pad

    lam_re = [lam_ref[0, v * SUBLANES:(v + 1) * SUBLANES, :] for v in range(2)]
    lam_im = [lam_ref[1, v * SUBLANES:(v + 1) * SUBLANES, :] for v in range(2)]
    sub = lax.broadcasted_iota(jnp.int32, (SUBLANES, LANES), 0)

    def rows(n, part, v):
        first = (part * half_slabs + v * SUBLANES) * pitch
        return pl.ds(first + SCAN_PAD - skew + n, SUBLANES, stride=pitch + 1)

    def step(n, state, valid=None):
        new = []
        for b in range(n_seq):
            for v in range(2):
                x_re, x_im = state[4 * b + 2 * v], state[4 * b + 2 * v + 1]
                b_re = slab_refs[b][rows(n, 0, v), :]
                b_im = slab_refs[b][rows(n, 1, v), :]
                n_re = lam_re[v] * x_re - lam_im[v] * x_im + b_re
                n_im = lam_re[v] * x_im + lam_im[v] * x_re + b_im
                if valid is not None:
                    n_re = jnp.where(valid, n_re, x_re)
                    n_im = jnp.where(valid, n_im, x_im)
                slab_refs[b][rows(n, 0, v), :] = n_re
                slab_refs[b][rows(n, 1, v), :] = n_im
                new += [n_re, n_im]
        return tuple(new)

    state = tuple(carry_ref[b, i * SUBLANES:(i + 1) * SUBLANES, :]
                  for b in range(n_seq) for i in range(4))
    for n in range(tile + skew):
        if n < skew:
            state = step(n, state, sub >= skew - n)
        elif n < tile:
            state = step(n, state)
        else:
            state = step(n, state, sub < tile + skew - n)
    for b in range(n_seq):
        for i in range(4):
            carry_ref[b, i * SUBLANES:(i + 1) * SUBLANES, :] = state[4 * b + i]

    for b in range(n_seq):
        ys = []
        for gh in range(2):
            acc = None
            for part in range(2):
                first = part * half_slabs + gh * quarter
                xs = jnp.concatenate(
                    [slab_refs[b][pl.ds((first + s) * pitch + SCAN_PAD, tile), :]
                     for s in range(quarter)], axis=1)
                term = _dot(xs.astype(BF16), wc_ref[part, gh])
                acc = term if part == 0 else acc - term
            ys.append(acc)
        y = jnp.concatenate(ys, axis=1) + d_ref[...] * u_ref[b]
        y = _gelu_tanh(y)
        gate = _dot(y.astype(BF16), gw_ref[...]) + gb_ref[...]
        o_ref[b] = (y * _sigmoid(gate)).astype(o_ref.dtype)


def _s5_mixer(u, lam, wb, wc, d_skip, glu_w, glu_b):
    bsz, seqlen, d_ssm = u.shape
    tile = SCAN_TILE
    n_seq = SCAN_BATCH
    n_slab = 2 * lam.shape[1]
    assert bsz % n_seq == 0 and seqlen % tile == 0
    full = lambda a: pl.BlockSpec(a.shape, lambda b, i: (0,) * a.ndim)
    return pl.pallas_call(
        functools.partial(_s5_kernel, n_slab=n_slab, n_seq=n_seq),
        grid=(bsz // n_seq, seqlen // tile),
        in_specs=[
            pl.BlockSpec((n_seq, tile, d_ssm), lambda b, i: (b, i, 0)),
            full(lam), full(wb), full(wc), full(d_skip), full(glu_w), full(glu_b),
        ],
        out_specs=pl.BlockSpec((n_seq, tile, d_ssm), lambda b, i: (b, i, 0)),
        out_shape=jax.ShapeDtypeStruct((bsz, seqlen, d_ssm), BF16),
        scratch_shapes=[pltpu.VMEM((n_slab * SCAN_PITCH, LANES), F32) for _ in range(n_seq)]
                       + [pltpu.VMEM((n_seq, n_slab, LANES), F32)],
        compiler_params=_params("arbitrary", "arbitrary"),
        name="s5_mixer",
    )(u, lam, wb, wc, d_skip, glu_w, glu_b)


def _attn_kernel(qt_ref, k_ref, vt_ref, lq1_ref, lk1_ref, lq2_ref, lk2_ref, sub_ref, o_ref,
                 p_ref, acc_ref, *, lam_init, heads):
    tq = qt_ref.shape[1]
    tk = ATT_K_TILE
    head_w = 2 * DIFF_HEAD
    qi = pl.program_id(2)
    ones = jnp.ones((BF16_ROWS, tk), BF16)

    rhs = []
    for h in range(heads):
        qt = qt_ref[h * head_w:(h + 1) * head_w, :]
        sub = lax.broadcasted_iota(jnp.int32, qt.shape, 0)
        zero = jnp.zeros_like(qt)
        rhs.append(jnp.concatenate([jnp.where(sub < DIFF_HEAD, qt, zero),
                                    jnp.where(sub >= DIFF_HEAD, qt, zero)], axis=1))

    def scores(j, h):
        start = pl.multiple_of(j * tk, tk)
        return _dot(k_ref[pl.ds(start, tk), h * head_w:(h + 1) * head_w], rhs[h])

    def pv(j, h):
        lhs = jnp.concatenate([vt_ref[j, h * head_w:(h + 1) * head_w, :], ones], axis=0)
        return _dot(lhs, p_ref[h])

    stats = []
    for h in range(heads):
        s = scores(qi, h)
        key = lax.broadcasted_iota(jnp.int32, s.shape, 0)
        qry = lax.broadcasted_iota(jnp.int32, s.shape, 1) % tq
        s = jnp.where(key <= qry, s, -jnp.inf)
        m = jnp.max(s, axis=0, keepdims=True)
        p_ref[h] = jnp.exp2(s - m).astype(BF16)
        acc_ref[h] = jnp.zeros(acc_ref.shape[1:], F32)
        stats.append((m, jnp.ones_like(m)))

    def step(j, pending, carry):
        out = []
        for h in range(heads):
            m, alpha_p = carry[h]
            acc_ref[h] = alpha_p * acc_ref[h] + pv(pending, h)
            s = scores(j, h)
            m_new = jnp.maximum(m, jnp.max(s, axis=0, keepdims=True))
            p_ref[h] = jnp.exp2(s - m_new).astype(BF16)
            out.append((m_new, jnp.exp2(m - m_new)))
        return tuple(out)

    def pair(i, carry):
        j = 2 * i
        carry = step(j, jnp.where(i == 0, qi, j - 1), carry)
        return step(j + 1, j, carry)

    carry = lax.fori_loop(0, qi // 2, pair, tuple(stats))
    last = qi - 1
    carry = lax.fori_loop(
        0, qi % 2, lambda _, c: step(last, jnp.where(last == 0, qi, last - 1), c), carry)
    pending = jnp.where(qi == 0, qi, last)

    lam = (jnp.exp(jnp.sum(lq1_ref[...] * lk1_ref[...], axis=-1, keepdims=True))
           - jnp.exp(jnp.sum(lq2_ref[...] * lk2_ref[...], axis=-1, keepdims=True)) + lam_init)
    gain = sub_ref[...] * (1.0 - lam_init)
    for h in range(heads):
        _, alpha_p = carry[h]
        acc = alpha_p * acc_ref[h] + pv(pending, h)
        inv = 1.0 / acc[head_w:head_w + 1, :]
        o = acc[:head_w, :tq] * inv[:, :tq] - acc[:head_w, tq:] * (lam * inv[:, tq:])
        o = o * lax.rsqrt(jnp.mean(o * o, axis=0, keepdims=True) + RMS_EPS) * gain
        o_ref[:, h * head_w:(h + 1) * head_w] = o.T.astype(o_ref.dtype)


def _diff_attention(qt, k, vt, lq1, lk1, lq2, lk2, subln_col, lam_init):
    bsz, seqlen, d_att = k.shape
    head_w = 2 * DIFF_HEAD
    heads = ATT_HEADS_PER_STEP
    width = heads * head_w
    assert ATT_Q_TILE == ATT_K_TILE and d_att % width == 0
    small = lambda a: pl.BlockSpec(a.shape, lambda b, h, i: (0, 0))
    return pl.pallas_call(
        functools.partial(_attn_kernel, lam_init=lam_init, heads=heads),
        grid=(bsz, d_att // width, seqlen // ATT_Q_TILE),
        in_specs=[
            pl.BlockSpec((pl.Squeezed(), width, ATT_Q_TILE), lambda b, h, i: (b, h, i)),
            pl.BlockSpec((pl.Squeezed(), seqlen, width), lambda b, h, i: (b, 0, h)),
            pl.BlockSpec((pl.Squeezed(), seqlen // ATT_K_TILE, width, ATT_K_TILE),
                         lambda b, h, i: (b, 0, h, 0)),
            small(lq1), small(lk1), small(lq2), small(lk2), small(subln_col),
        ],
        out_specs=pl.BlockSpec((pl.Squeezed(), ATT_Q_TILE, width), lambda b, h, i: (b, i, h)),
        out_shape=jax.ShapeDtypeStruct((bsz, seqlen, d_att), BF16),
        scratch_shapes=[
            pltpu.VMEM((heads, ATT_K_TILE, 2 * ATT_Q_TILE), BF16),
            pltpu.VMEM((heads, head_w + BF16_ROWS, 2 * ATT_Q_TILE), F32),
        ],
        compiler_params=_params("arbitrary", "arbitrary", "arbitrary"),
        name="diff_attention",
    )(qt, k, vt, lq1, lk1, lq2, lk2, subln_col)


def _mix_ffn_kernel(x_ref, ssm_ref, att_ref, mod_ref, wo_ref, g1_ref, b1_ref,
                    wg_ref, wu_ref, wd_ref, g2_ref, b2_ref, o_ref, *, alpha, d_ssm):
    gate1 = mod_ref[2:3, :]
    shift2 = mod_ref[3:4, :]
    scale2 = mod_ref[4:5, :]
    gate2 = mod_ref[5:6, :]
    mix = _dot(ssm_ref[...], wo_ref[:d_ssm, :]) + _dot(att_ref[...], wo_ref[d_ssm:, :])
    x1 = _layer_norm(alpha * x_ref[...] + (1.0 + gate1) * mix, g1_ref[...], b1_ref[...])
    h = (x1 * (1.0 + scale2) + shift2).astype(BF16)
    a = _silu(_dot(h, wg_ref[...])) * _dot(h, wu_ref[...])
    ffn = _dot(a.astype(BF16), wd_ref[...])
    o_ref[...] = _layer_norm(alpha * x1 + (1.0 + gate2) * ffn, g2_ref[...], b2_ref[...])


def _mix_ffn(x, ssm, att, mod, w_out, g1, b1, wg, wu, wd, g2, b2, alpha):
    bsz, seqlen, d = x.shape
    rows = ROW_TILE
    d_ssm = ssm.shape[-1]
    tok = lambda width: pl.BlockSpec((pl.Squeezed(), rows, width), lambda bb, i: (bb, i, 0))
    resident = lambda a: pl.BlockSpec(a.shape, lambda bb, i: (0,) * a.ndim,
                                      pipeline_mode=pl.Buffered(1))
    return pl.pallas_call(
        functools.partial(_mix_ffn_kernel, alpha=alpha, d_ssm=d_ssm),
        grid=(bsz, seqlen // rows),
        in_specs=[
            tok(d), tok(d_ssm), tok(att.shape[-1]),
            pl.BlockSpec((pl.Squeezed(), mod.shape[1], d), lambda bb, i: (bb, 0, 0)),
            resident(w_out), resident(g1), resident(b1),
            resident(wg), resident(wu), resident(wd), resident(g2), resident(b2),
        ],
        out_specs=tok(d),
        out_shape=jax.ShapeDtypeStruct(x.shape, F32),
        compiler_params=_params("arbitrary", "arbitrary"),
        name="out_proj_swiglu_ln",
    )(x, ssm, att, mod, w_out, g1, b1, wg, wu, wd, g2, b2)


def kernel(x, c, positions, mod_w, mod_b, w_in, ssm_a_re, ssm_a_im, ssm_log_step, ssm_b_re, ssm_b_im, ssm_c_re, ssm_c_im, ssm_d, glu_w, glu_b, lam_q1, lam_k1, lam_q2, lam_k2, subln_w, w_out, ln1_g, ln1_b, ffn_w_gate, ffn_w_up, ffn_w_down, ln2_g, ln2_b):
    bsz, seqlen, d = x.shape
    depth = mod_w.shape[0]
    n_groups = ssm_a_re.shape[1]
    d_ssm = n_groups * SSM_GROUP
    d_att = (w_in.shape[2] - d_ssm) // 3
    alpha = (2 * depth) ** 0.25

    mod = _modulation(c, mod_w, mod_b).reshape(depth, bsz, 6, d)
    ropes = _rope_tables(positions)
    lb_re, lb_im, bb_re, bb_im = _discretise(ssm_a_re, ssm_a_im, ssm_log_step, ssm_b_re, ssm_b_im)
    n_lane_rows = n_groups * SSM_STATE // LANES
    lam = jnp.stack([lb_re.reshape(depth, n_lane_rows, LANES),
                     lb_im.reshape(depth, n_lane_rows, LANES)], axis=1)
    wb = jnp.stack([_block_diag_in(bb_re), _block_diag_in(bb_im)], axis=1).astype(BF16)
    wc = jnp.stack([_block_diag_out(ssm_c_re), _block_diag_out(ssm_c_im)], axis=1).astype(BF16)
    d_skip = ssm_d.reshape(depth, 1, d_ssm)
    row = lambda a: a.reshape(depth, 1, a.shape[-1])

    w_in_b = w_in.astype(BF16)
    glu_w_b = glu_w.astype(BF16)
    w_out_b = w_out.astype(BF16)
    wg_b = ffn_w_gate.astype(BF16)
    wu_b = ffn_w_up.astype(BF16)
    wd_b = ffn_w_down.astype(BF16)

    for l in range(depth):
        lam_init = 0.8 - 0.6 * math.exp(-0.3 * l)
        u, qt, k, vt = _in_proj(x, mod[l], w_in_b[l], ropes, d_ssm, d_att)
        ssm = _s5_mixer(u, lam[l], wb[l], wc[l], d_skip[l], glu_w_b[l], row(glu_b)[l])
        att = _diff_attention(qt, k, vt, row(lam_q1)[l], row(lam_k1)[l], row(lam_q2)[l],
                              row(lam_k2)[l], subln_w[l].reshape(-1, 1), lam_init)
        x = _mix_ffn(x, ssm, att, mod[l], w_out_b[l], row(ln1_g)[l], row(ln1_b)[l],
                     wg_b[l], wu_b[l], wd_b[l], row(ln2_g)[l], row(ln2_b)[l], alpha)
    return x
```

```python
import functools
import math

import jax
import jax.numpy as jnp
from jax import lax
from jax.experimental import pallas as pl
from jax.experimental.pallas import tpu as pltpu

F32 = jnp.float32
BF16 = jnp.bfloat16

SSM_GROUP = 16
SSM_STATE = 64
DIFF_HEAD = 64
ROT_DIM = DIFF_HEAD // 4
ROPE_THETA = 500000.0
LN_EPS = 1e-5
RMS_EPS = 1e-5

LANES = 128
SUBLANES = 8
BF16_ROWS = 16
VMEM_LIMIT_BYTES = 56 * 1024 * 1024

ROW_TILE = 512
SCAN_TILE = 256
SCAN_PAD = SUBLANES
SCAN_PITCH = SCAN_TILE + 2 * SCAN_PAD
SCAN_BATCH = 2
ATT_Q_TILE = 256
ATT_K_TILE = 256
ATT_HEADS_PER_STEP = 4
ATT_SEQS_PER_STEP = 4


def _sigmoid(x):
    return 1.0 / (1.0 + jnp.exp(-x))


def _silu(x):
    return x * _sigmoid(x)


def _gelu_tanh(x):
    c = math.sqrt(2.0 / math.pi)
    return 0.5 * x * (1.0 + jnp.tanh(c * (x + 0.044715 * (x * x * x))))


def _dot(a, b):
    return jnp.dot(a, b, preferred_element_type=F32)


def _layer_norm(y, g, b):
    mu = jnp.mean(y, axis=-1, keepdims=True)
    yc = y - mu
    var = jnp.mean(yc * yc, axis=-1, keepdims=True)
    return yc * lax.rsqrt(var + LN_EPS) * g + b


def _layer_spec(a, layer):
    zeros = (0,) * (a.ndim - 1)
    return pl.BlockSpec((pl.Squeezed(),) + a.shape[1:], lambda *_: (layer,) + zeros,
                        pipeline_mode=pl.Buffered(1))


def _mod_spec(mod, layer):
    return pl.BlockSpec((pl.Squeezed(), pl.Squeezed()) + mod.shape[2:],
                        lambda b, *_: (layer, b, 0, 0))


def _params(*sem):
    return pltpu.CompilerParams(dimension_semantics=sem, vmem_limit_bytes=VMEM_LIMIT_BYTES)


def _mod_kernel(c_ref, w_ref, b_ref, o_ref):
    cond = _silu(c_ref[...])
    o_ref[...] = _dot(cond.astype(BF16), w_ref[...].astype(BF16)) + b_ref[...]


def _modulation(c, mod_w, mod_b):
    depth, d, six_d = mod_w.shape
    bsz = c.shape[0]
    n_col = six_d // d
    return pl.pallas_call(
        _mod_kernel,
        grid=(depth, n_col),
        in_specs=[
            pl.BlockSpec((bsz, d), lambda l, j: (0, 0)),
            pl.BlockSpec((pl.Squeezed(), d, d), lambda l, j: (l, 0, j)),
            pl.BlockSpec((pl.Squeezed(), 1, d), lambda l, j: (l, 0, j)),
        ],
        out_specs=pl.BlockSpec((pl.Squeezed(), bsz, d), lambda l, j: (l, 0, j)),
        out_shape=jax.ShapeDtypeStruct((depth, bsz, six_d), F32),
        compiler_params=_params("arbitrary", "arbitrary"),
        name="adaln_modulation",
    )(c, mod_w, mod_b.reshape(depth, 1, six_d))


def _rope_kernel(pos_ref, freq_ref, cos_ref, sina_ref, sinb_ref):
    ang = pos_ref[...] * freq_ref[...]
    lane = lax.broadcasted_iota(jnp.int32, ang.shape, 1) % DIFF_HEAD
    c = jnp.cos(ang)
    s = jnp.sin(ang)
    half = ROT_DIM // 2
    cos_ref[...] = jnp.where(lane < ROT_DIM, c, 1.0)
    sina_ref[...] = jnp.where(lane < half, -s, 0.0)
    sinb_ref[...] = jnp.where((lane >= half) & (lane < ROT_DIM), s, 0.0)


def _rope_tables(positions):
    bsz, seqlen = positions.shape
    half = ROT_DIM // 2
    freqs = ROPE_THETA ** (-jnp.arange(0, ROT_DIM, 2, dtype=F32) / ROT_DIM)
    lane = jnp.arange(LANES) % DIFF_HEAD
    freq_lane = jnp.where(lane < ROT_DIM, freqs[lane % half], 0.0).reshape(1, LANES)
    rows = ROW_TILE
    spec = pl.BlockSpec((pl.Squeezed(), rows, LANES), lambda b, i: (b, i, 0))
    shape = jax.ShapeDtypeStruct((bsz, seqlen, LANES), F32)
    return pl.pallas_call(
        _rope_kernel,
        grid=(bsz, seqlen // rows),
        in_specs=[
            spec,
            pl.BlockSpec((1, LANES), lambda b, i: (0, 0)),
        ],
        out_specs=[spec, spec, spec],
        out_shape=[shape, shape, shape],
        compiler_params=_params("arbitrary", "arbitrary"),
        name="rope_tables",
    )(jnp.broadcast_to(positions.astype(F32)[:, :, None], (bsz, seqlen, LANES)), freq_lane)


def _disc_kernel(are_ref, aim_ref, ls_ref, bre_ref, bim_ref,
                 lbre_ref, lbim_ref, bbre_ref, bbim_ref):
    delta = jnp.exp(ls_ref[...])
    lam_re = jnp.minimum(are_ref[...], -1e-4)
    lam_im = aim_ref[...]
    mag = jnp.exp(lam_re * delta)
    ang = lam_im * delta
    lb_re = mag * jnp.cos(ang)
    lb_im = mag * jnp.sin(ang)
    den = lam_re * lam_re + lam_im * lam_im
    n_re = lb_re - 1.0
    n_im = lb_im
    k_re = (n_re * lam_re + n_im * lam_im) / den
    k_im = (n_im * lam_re - n_re * lam_im) / den
    b_re = bre_ref[...]
    b_im = bim_ref[...]
    lbre_ref[...] = lb_re
    lbim_ref[...] = lb_im
    bbre_ref[...] = k_re * b_re - k_im * b_im
    bbim_ref[...] = k_re * b_im + k_im * b_re


def _discretise(a_re, a_im, log_step, b_re, b_im):
    depth, g, p = a_re.shape
    c = b_re.shape[-1]
    rows = depth * g * c

    def rep(a):
        return jnp.broadcast_to(a[:, :, None, :], (depth, g, c, p)).reshape(rows, p)

    ls = jnp.broadcast_to(log_step[:, :, None, None], (depth, g, c, 1)).reshape(rows, 1)
    bt_re = b_re.transpose(0, 1, 3, 2).reshape(rows, p)
    bt_im = b_im.transpose(0, 1, 3, 2).reshape(rows, p)
    shape = jax.ShapeDtypeStruct((rows, p), F32)
    outs = pl.pallas_call(
        _disc_kernel,
        out_shape=[shape, shape, shape, shape],
        name="s5_discretise",
    )(rep(a_re), rep(a_im), ls, bt_re, bt_im)
    lb_re, lb_im, bb_re, bb_im = [o.reshape(depth, g, c, p) for o in outs]
    return lb_re[:, :, 0, :], lb_im[:, :, 0, :], bb_re, bb_im


def _block_diag_in(w):
    depth, g, c, p = w.shape
    gh = g // 2
    eye = jnp.eye(gh, dtype=w.dtype)
    w = w.reshape(depth, 2, gh, c, p)
    return jnp.einsum("lhgcp,gk->lhgckp", w, eye).reshape(depth, 2, gh * c, gh * p)


def _block_diag_out(w):
    depth, g, c, p = w.shape
    gh = g // 2
    eye = jnp.eye(gh, dtype=w.dtype)
    w = w.reshape(depth, 2, gh, c, p)
    return jnp.einsum("lhgcp,gk->lhgpkc", w, eye).reshape(depth, 2, gh * p, gh * c)


def _in_proj_kernel(x_ref, mod_ref, w_ref, cos_ref, sina_ref, sinb_ref,
                    u_ref, qt_ref, k_ref, vt_ref, *, d_ssm, d_att):
    shift = mod_ref[0:1, :]
    scale = mod_ref[1:2, :]
    h = x_ref[...] * (1.0 + scale) + shift
    proj = _dot(h.astype(BF16), w_ref[...])
    u_ref[...] = proj[:, :d_ssm]
    cos_t = cos_ref[...]
    sin_a = sina_ref[...]
    sin_b = sinb_ref[...]
    half = ROT_DIM // 2

    def rotary(t):
        return t * cos_t + pltpu.roll(t, LANES - half, 1) * sin_a + pltpu.roll(t, half, 1) * sin_b

    q_scale = DIFF_HEAD ** -0.5 * math.log2(math.e)
    for blk in range(d_att // LANES):
        cols = slice(blk * LANES, (blk + 1) * LANES)
        lo = d_ssm + blk * LANES
        qt_ref[cols, :] = (rotary(proj[:, lo:lo + LANES]) * q_scale).T.astype(BF16)
        lo = d_ssm + d_att + blk * LANES
        k_ref[:, cols] = rotary(proj[:, lo:lo + LANES]).astype(BF16)
        lo = d_ssm + 2 * d_att + blk * LANES
        vt = proj[:, lo:lo + LANES].T.astype(BF16)
        for kb in range(vt_ref.shape[0]):
            vt_ref[kb, cols, :] = vt[:, kb * ATT_K_TILE:(kb + 1) * ATT_K_TILE]


def _in_proj(x, mod, w_in, ropes, layer, d_ssm, d_att):
    bsz, seqlen, d = x.shape
    rows = ROW_TILE
    kt = rows // ATT_K_TILE
    tok = lambda width: pl.BlockSpec((pl.Squeezed(), rows, width), lambda b, i: (b, i, 0))
    return pl.pallas_call(
        functools.partial(_in_proj_kernel, d_ssm=d_ssm, d_att=d_att),
        grid=(bsz, seqlen // rows),
        in_specs=[
            tok(d), _mod_spec(mod, layer), _layer_spec(w_in, layer),
            tok(LANES), tok(LANES), tok(LANES),
        ],
        out_specs=[
            tok(d_ssm),
            pl.BlockSpec((pl.Squeezed(), d_att, rows), lambda b, i: (b, 0, i)),
            tok(d_att),
            pl.BlockSpec((pl.Squeezed(), kt, d_att, ATT_K_TILE), lambda b, i: (b, i, 0, 0)),
        ],
        out_shape=[
            jax.ShapeDtypeStruct((bsz, seqlen, d_ssm), F32),
            jax.ShapeDtypeStruct((bsz, d_att, seqlen), BF16),
            jax.ShapeDtypeStruct((bsz, seqlen, d_att), BF16),
            jax.ShapeDtypeStruct((bsz, seqlen // ATT_K_TILE, d_att, ATT_K_TILE), BF16),
        ],
        compiler_params=_params("arbitrary", "arbitrary"),
        name="in_proj_rotary",
    )(x, mod, w_in, *ropes)


def _s5_kernel(u_ref, lam_ref, wb_ref, wc_ref, d_ref, gw_ref, gb_ref, o_ref,
               *scratch, n_slab, n_seq):
    slab_refs = scratch[:n_seq]
    carry_ref = scratch[n_seq]
    tile = u_ref.shape[1]
    pitch = SCAN_PITCH
    half_slabs = n_slab // 2
    quarter = half_slabs // 2
    skew = SUBLANES - 1

    @pl.when(pl.program_id(1) == 0)
    def _():
        carry_ref[...] = jnp.zeros_like(carry_ref)

    half_w = u_ref.shape[2] // 2
    for b in range(n_seq):
        ub = u_ref[b].astype(BF16)
        pad = jnp.zeros((SCAN_PAD, LANES), F32)
        for part in range(2):
            for gh in range(2):
                r = _dot(ub[:, gh * half_w:(gh + 1) * half_w], wb_ref[part, gh])
                for s in range(quarter):
                    base = (part * half_slabs + gh * quarter + s) * pitch
                    slab_refs[b][pl.ds(base, SCAN_PAD), :] = pad
                    slab_refs[b][pl.ds(base + SCAN_PAD, tile), :] = r[:, s * LANES:(s + 1) * LANES]
                    slab_refs[b][pl.ds(base + SCAN_PAD + tile, SCAN_PAD), :] = pad

    lam_re = [lam_ref[0, v * SUBLANES:(v + 1) * SUBLANES, :] for v in range(2)]
    lam_im = [lam_ref[1, v * SUBLANES:(v + 1) * SUBLANES, :] for v in range(2)]
    sub = lax.broadcasted_iota(jnp.int32, (SUBLANES, LANES), 0)

    def rows(n, part, v):
        first = (part * half_slabs + v * SUBLANES) * pitch
        return pl.ds(first + SCAN_PAD - skew + n, SUBLANES, stride=pitch + 1)

    def step(n, state, valid=None):
        new = []
        for b in range(n_seq):
            for v in range(2):
                x_re, x_im = state[4 * b + 2 * v], state[4 * b + 2 * v + 1]
                b_re = slab_refs[b][rows(n, 0, v), :]
                b_im = slab_refs[b][rows(n, 1, v), :]
                n_re = lam_re[v] * x_re - lam_im[v] * x_im + b_re
                n_im = lam_re[v] * x_im + lam_im[v] * x_re + b_im
                if valid is not None:
                    n_re = jnp.where(valid, n_re, x_re)
                    n_im = jnp.where(valid, n_im, x_im)
                slab_refs[b][rows(n, 0, v), :] = n_re
                slab_refs[b][rows(n, 1, v), :] = n_im
                new += [n_re, n_im]
        return tuple(new)

    state = tuple(carry_ref[b, i * SUBLANES:(i + 1) * SUBLANES, :]
                  for b in range(n_seq) for i in range(4))
    for n in range(tile + skew):
        if n < skew:
            state = step(n, state, sub >= skew - n)
        elif n < tile:
            state = step(n, state)
        else:
            state = step(n, state, sub < tile + skew - n)
    for b in range(n_seq):
        for i in range(4):
            carry_ref[b, i * SUBLANES:(i + 1) * SUBLANES, :] = state[4 * b + i]

    for b in range(n_seq):
        ys = []
        for gh in range(2):
            acc = None
            for part in range(2):
                first = part * half_slabs + gh * quarter
                xs = jnp.concatenate(
                    [slab_refs[b][pl.ds((first + s) * pitch + SCAN_PAD, tile), :]
                     for s in range(quarter)], axis=1)
                term = _dot(xs.astype(BF16), wc_ref[part, gh])
                acc = term if part == 0 else acc - term
            ys.append(acc)
        y = jnp.concatenate(ys, axis=1) + d_ref[...] * u_ref[b]
        y = _gelu_tanh(y)
        gate = _dot(y.astype(BF16), gw_ref[...]) + gb_ref[...]
        o_ref[b] = (y * _sigmoid(gate)).astype(o_ref.dtype)


def _s5_mixer(u, lam, wb, wc, d_skip, glu_w, glu_b, layer):
    bsz, seqlen, d_ssm = u.shape
    tile = SCAN_TILE
    n_seq = SCAN_BATCH
    n_slab = 2 * lam.shape[2]
    assert bsz % n_seq == 0 and seqlen % tile == 0
    full = lambda a: _layer_spec(a, layer)
    return pl.pallas_call(
        functools.partial(_s5_kernel, n_slab=n_slab, n_seq=n_seq),
        grid=(bsz // n_seq, seqlen // tile),
        in_specs=[
            pl.BlockSpec((n_seq, tile, d_ssm), lambda b, i: (b, i, 0)),
            full(lam), full(wb), full(wc), full(d_skip), full(glu_w), full(glu_b),
        ],
        out_specs=pl.BlockSpec((n_seq, tile, d_ssm), lambda b, i: (b, i, 0)),
        out_shape=jax.ShapeDtypeStruct((bsz, seqlen, d_ssm), BF16),
        scratch_shapes=[pltpu.VMEM((n_slab * SCAN_PITCH, LANES), F32) for _ in range(n_seq)]
                       + [pltpu.VMEM((n_seq, n_slab, LANES), F32)],
        compiler_params=_params("arbitrary", "arbitrary"),
        name="s5_mixer",
    )(u, lam, wb, wc, d_skip, glu_w, glu_b)


def _attn_kernel(qt_ref, k_ref, vt_ref, lq1_ref, lk1_ref, lq2_ref, lk2_ref, sub_ref, o_ref,
                 p_ref, acc_ref, *, lam_init, heads, seqs):
    tq = qt_ref.shape[2]
    tk = ATT_K_TILE
    head_w = 2 * DIFF_HEAD
    qi = pl.program_id(2)
    chains = [(b, h) for b in range(seqs) for h in range(heads)]
    ones = jnp.ones((BF16_ROWS, tk), BF16)

    rhs = []
    for b, h in chains:
        qt = qt_ref[b, h * head_w:(h + 1) * head_w, :]
        sub = lax.broadcasted_iota(jnp.int32, qt.shape, 0)
        zero = jnp.zeros_like(qt)
        rhs.append(jnp.concatenate([jnp.where(sub < DIFF_HEAD, qt, zero),
                                    jnp.where(sub >= DIFF_HEAD, qt, zero)], axis=1))

    def scores(j, c):
        b, h = chains[c]
        start = pl.multiple_of(j * tk, tk)
        return _dot(k_ref[b, pl.ds(start, tk), h * head_w:(h + 1) * head_w], rhs[c])

    def pv(j, c):
        b, h = chains[c]
        lhs = jnp.concatenate([vt_ref[b, j, h * head_w:(h + 1) * head_w, :], ones], axis=0)
        return _dot(lhs, p_ref[c])

    stats = []
    for c in range(len(chains)):
        s = scores(qi, c)
        key = lax.broadcasted_iota(jnp.int32, s.shape, 0)
        qry = lax.broadcasted_iota(jnp.int32, s.shape, 1) % tq
        s = jnp.where(key <= qry, s, -jnp.inf)
        m = jnp.max(s, axis=0, keepdims=True)
        p_ref[c] = jnp.exp2(s - m).astype(BF16)
        acc_ref[c] = jnp.zeros(acc_ref.shape[1:], F32)
        stats.append((m, jnp.ones_like(m)))

    def step(j, pending, carry):
        out = []
        for c in range(len(chains)):
            m, alpha_p = carry[c]
            acc_ref[c] = alpha_p * acc_ref[c] + pv(pending, c)
            s = scores(j, c)
            m_new = jnp.maximum(m, jnp.max(s, axis=0, keepdims=True))
            p_ref[c] = jnp.exp2(s - m_new).astype(BF16)
            out.append((m_new, jnp.exp2(m - m_new)))
        return tuple(out)

    def pair(i, carry):
        j = 2 * i
        carry = step(j, jnp.where(i == 0, qi, j - 1), carry)
        return step(j + 1, j, carry)

    carry = lax.fori_loop(0, qi // 2, pair, tuple(stats))
    last = qi - 1
    carry = lax.fori_loop(
        0, qi % 2, lambda _, c: step(last, jnp.where(last == 0, qi, last - 1), c), carry)
    pending = jnp.where(qi == 0, qi, last)

    lam = (jnp.exp(jnp.sum(lq1_ref[...] * lk1_ref[...], axis=-1, keepdims=True))
           - jnp.exp(jnp.sum(lq2_ref[...] * lk2_ref[...], axis=-1, keepdims=True)) + lam_init)
    gain = sub_ref[...] * (1.0 - lam_init)
    for c, (b, h) in enumerate(chains):
        _, alpha_p = carry[c]
        acc = alpha_p * acc_ref[c] + pv(pending, c)
        inv = 1.0 / acc[head_w:head_w + 1, :]
        o = acc[:head_w, :tq] * inv[:, :tq] - acc[:head_w, tq:] * (lam * inv[:, tq:])
        o = o * lax.rsqrt(jnp.mean(o * o, axis=0, keepdims=True) + RMS_EPS) * gain
        o_ref[b, :, h * head_w:(h + 1) * head_w] = o.T.astype(o_ref.dtype)


def _diff_attention(qt, k, vt, lq1, lk1, lq2, lk2, subln_col, layer, lam_init):
    bsz, seqlen, d_att = k.shape
    head_w = 2 * DIFF_HEAD
    heads = ATT_HEADS_PER_STEP
    seqs = ATT_SEQS_PER_STEP
    width = heads * head_w
    assert ATT_Q_TILE == ATT_K_TILE and d_att % width == 0 and bsz % seqs == 0
    small = lambda a: _layer_spec(a, layer)
    return pl.pallas_call(
        functools.partial(_attn_kernel, lam_init=lam_init, heads=heads, seqs=seqs),
        grid=(bsz // seqs, d_att // width, seqlen // ATT_Q_TILE),
        in_specs=[
            pl.BlockSpec((seqs, width, ATT_Q_TILE), lambda b, h, i: (b, h, i)),
            pl.BlockSpec((seqs, seqlen, width), lambda b, h, i: (b, 0, h)),
            pl.BlockSpec((seqs, seqlen // ATT_K_TILE, width, ATT_K_TILE),
                         lambda b, h, i: (b, 0, h, 0)),
            small(lq1), small(lk1), small(lq2), small(lk2), small(subln_col),
        ],
        out_specs=pl.BlockSpec((seqs, ATT_Q_TILE, width), lambda b, h, i: (b, i, h)),
        out_shape=jax.ShapeDtypeStruct((bsz, seqlen, d_att), BF16),
        scratch_shapes=[
            pltpu.VMEM((seqs * heads, ATT_K_TILE, 2 * ATT_Q_TILE), BF16),
            pltpu.VMEM((seqs * heads, head_w + BF16_ROWS, 2 * ATT_Q_TILE), F32),
        ],
        compiler_params=_params("arbitrary", "arbitrary", "arbitrary"),
        name="diff_attention",
    )(qt, k, vt, lq1, lk1, lq2, lk2, subln_col)


def _mix_ffn_kernel(x_ref, ssm_ref, att_ref, mod_ref, wo_ref, g1_ref, b1_ref,
                    wg_ref, wu_ref, wd_ref, g2_ref, b2_ref, o_ref, *, alpha, d_ssm):
    gate1 = mod_ref[2:3, :]
    shift2 = mod_ref[3:4, :]
    scale2 = mod_ref[4:5, :]
    gate2 = mod_ref[5:6, :]
    mix = _dot(ssm_ref[...], wo_ref[:d_ssm, :]) + _dot(att_ref[...], wo_ref[d_ssm:, :])
    x1 = _layer_norm(alpha * x_ref[...] + (1.0 + gate1) * mix, g1_ref[...], b1_ref[...])
    h = (x1 * (1.0 + scale2) + shift2).astype(BF16)
    a = _silu(_dot(h, wg_ref[...])) * _dot(h, wu_ref[...])
    ffn = _dot(a.astype(BF16), wd_ref[...])
    o_ref[...] = _layer_norm(alpha * x1 + (1.0 + gate2) * ffn, g2_ref[...], b2_ref[...])


def _mix_ffn(x, ssm, att, mod, w_out, g1, b1, wg, wu, wd, g2, b2, layer, alpha):
    bsz, seqlen, d = x.shape
    rows = ROW_TILE
    d_ssm = ssm.shape[-1]
    tok = lambda width: pl.BlockSpec((pl.Squeezed(), rows, width), lambda bb, i: (bb, i, 0))
    resident = lambda a: _layer_spec(a, layer)
    return pl.pallas_call(
        functools.partial(_mix_ffn_kernel, alpha=alpha, d_ssm=d_ssm),
        grid=(bsz, seqlen // rows),
        in_specs=[
            tok(d), tok(d_ssm), tok(att.shape[-1]), _mod_spec(mod, layer),
            resident(w_out), resident(g1), resident(b1),
            resident(wg), resident(wu), resident(wd), resident(g2), resident(b2),
        ],
        out_specs=tok(d),
        out_shape=jax.ShapeDtypeStruct(x.shape, F32),
        compiler_params=_params("arbitrary", "arbitrary"),
        name="out_proj_swiglu_ln",
    )(x, ssm, att, mod, w_out, g1, b1, wg, wu, wd, g2, b2)


def kernel(x, c, positions, mod_w, mod_b, w_in, ssm_a_re, ssm_a_im, ssm_log_step, ssm_b_re, ssm_b_im, ssm_c_re, ssm_c_im, ssm_d, glu_w, glu_b, lam_q1, lam_k1, lam_q2, lam_k2, subln_w, w_out, ln1_g, ln1_b, ffn_w_gate, ffn_w_up, ffn_w_down, ln2_g, ln2_b):
    bsz, seqlen, d = x.shape
    depth = mod_w.shape[0]
    n_groups = ssm_a_re.shape[1]
    d_ssm = n_groups * SSM_GROUP
    d_att = (w_in.shape[2] - d_ssm) // 3
    alpha = (2 * depth) ** 0.25

    mod = _modulation(c, mod_w, mod_b).reshape(depth, bsz, 6, d)
    ropes = _rope_tables(positions)
    lb_re, lb_im, bb_re, bb_im = _discretise(ssm_a_re, ssm_a_im, ssm_log_step, ssm_b_re, ssm_b_im)
    n_lane_rows = n_groups * SSM_STATE // LANES
    lam = jnp.stack([lb_re.reshape(depth, n_lane_rows, LANES),
                     lb_im.reshape(depth, n_lane_rows, LANES)], axis=1)
    wb = jnp.stack([_block_diag_in(bb_re), _block_diag_in(bb_im)], axis=1).astype(BF16)
    wc = jnp.stack([_block_diag_out(ssm_c_re), _block_diag_out(ssm_c_im)], axis=1).astype(BF16)
    d_skip = ssm_d.reshape(depth, 1, d_ssm)
    row = lambda a: a.reshape(depth, 1, a.shape[-1])
    subln_col = subln_w.reshape(depth, -1, 1)

    w_in_b = w_in.astype(BF16)
    glu_w_b = glu_w.astype(BF16)
    w_out_b = w_out.astype(BF16)
    wg_b = ffn_w_gate.astype(BF16)
    wu_b = ffn_w_up.astype(BF16)
    wd_b = ffn_w_down.astype(BF16)
    glu_b_r, lq1, lk1, lq2, lk2 = row(glu_b), row(lam_q1), row(lam_k1), row(lam_q2), row(lam_k2)
    g1, b1, g2, b2 = row(ln1_g), row(ln1_b), row(ln2_g), row(ln2_b)

    for l in range(depth):
        lam_init = 0.8 - 0.6 * math.exp(-0.3 * l)
        u, qt, k, vt = _in_proj(x, mod, w_in_b, ropes, l, d_ssm, d_att)
        ssm = _s5_mixer(u, lam, wb, wc, d_skip, glu_w_b, glu_b_r, l)
        att = _diff_attention(qt, k, vt, lq1, lk1, lq2, lk2, subln_col, l, lam_init)
        x = _mix_ffn(x, ssm, att, mod, w_out_b, g1, b1, wg_b, wu_b, wd_b, g2, b2, l, alpha)
    return x
```

```python
import functools
import math

import jax
import jax.numpy as jnp
from jax import lax
from jax.experimental import pallas as pl
from jax.experimental.pallas import tpu as pltpu

F32 = jnp.float32
BF16 = jnp.bfloat16

SSM_GROUP = 16
SSM_STATE = 64
DIFF_HEAD = 64
ROT_DIM = DIFF_HEAD // 4
ROPE_THETA = 500000.0
LN_EPS = 1e-5
RMS_EPS = 1e-5

LANES = 128
SUBLANES = 8
BF16_ROWS = 16
VMEM_LIMIT_BYTES = 56 * 1024 * 1024

ROW_TILE = 512
FFN_ROW_SPLIT = (192, 192, 128)
SCAN_TILE = 256
SCAN_PAD = SUBLANES
SCAN_PITCH = SCAN_TILE + 2 * SCAN_PAD
SCAN_BATCH = 2
ATT_Q_TILE = 256
ATT_K_TILE = 256
ATT_HEADS_PER_STEP = 4
ATT_SEQS_PER_STEP = 4


def _sigmoid(x):
    return 1.0 / (1.0 + jnp.exp(-x))


def _silu(x):
    return x * _sigmoid(x)


def _gelu_tanh(x):
    c = math.sqrt(2.0 / math.pi)
    return 0.5 * x * (1.0 + jnp.tanh(c * (x + 0.044715 * (x * x * x))))


def _dot(a, b):
    return jnp.dot(a, b, preferred_element_type=F32)


def _layer_norm(y, g, b):
    mu = jnp.mean(y, axis=-1, keepdims=True)
    yc = y - mu
    var = jnp.mean(yc * yc, axis=-1, keepdims=True)
    return yc * lax.rsqrt(var + LN_EPS) * g + b


def _row_blocks(sizes, total):
    assert sum(sizes) == total
    starts = [sum(sizes[:i]) for i in range(len(sizes))]
    return [slice(s, s + n) for s, n in zip(starts, sizes)]


def _layer_spec(a, layer):
    zeros = (0,) * (a.ndim - 1)
    return pl.BlockSpec((pl.Squeezed(),) + a.shape[1:], lambda *_: (layer,) + zeros,
                        pipeline_mode=pl.Buffered(1))


def _mod_spec(mod, layer):
    return pl.BlockSpec((pl.Squeezed(), pl.Squeezed()) + mod.shape[2:],
                        lambda b, *_: (layer, b, 0, 0))


def _params(*sem):
    return pltpu.CompilerParams(dimension_semantics=sem, vmem_limit_bytes=VMEM_LIMIT_BYTES)


def _mod_kernel(c_ref, w_ref, b_ref, o_ref):
    cond = _silu(c_ref[...])
    o_ref[...] = _dot(cond.astype(BF16), w_ref[...].astype(BF16)) + b_ref[...]


def _modulation(c, mod_w, mod_b):
    depth, d, six_d = mod_w.shape
    bsz = c.shape[0]
    n_col = six_d // d
    return pl.pallas_call(
        _mod_kernel,
        grid=(depth, n_col),
        in_specs=[
            pl.BlockSpec((bsz, d), lambda l, j: (0, 0)),
            pl.BlockSpec((pl.Squeezed(), d, d), lambda l, j: (l, 0, j)),
            pl.BlockSpec((pl.Squeezed(), 1, d), lambda l, j: (l, 0, j)),
        ],
        out_specs=pl.BlockSpec((pl.Squeezed(), bsz, d), lambda l, j: (l, 0, j)),
        out_shape=jax.ShapeDtypeStruct((depth, bsz, six_d), F32),
        compiler_params=_params("arbitrary", "arbitrary"),
        name="adaln_modulation",
    )(c, mod_w, mod_b.reshape(depth, 1, six_d))


def _rope_kernel(pos_ref, freq_ref, cos_ref, sina_ref, sinb_ref):
    ang = pos_ref[...] * freq_ref[...]
    lane = lax.broadcasted_iota(jnp.int32, ang.shape, 1) % DIFF_HEAD
    c = jnp.cos(ang)
    s = jnp.sin(ang)
    half = ROT_DIM // 2
    cos_ref[...] = jnp.where(lane < ROT_DIM, c, 1.0)
    sina_ref[...] = jnp.where(lane < half, -s, 0.0)
    sinb_ref[...] = jnp.where((lane >= half) & (lane < ROT_DIM), s, 0.0)


def _rope_tables(positions):
    bsz, seqlen = positions.shape
    half = ROT_DIM // 2
    freqs = ROPE_THETA ** (-jnp.arange(0, ROT_DIM, 2, dtype=F32) / ROT_DIM)
    lane = jnp.arange(LANES) % DIFF_HEAD
    freq_lane = jnp.where(lane < ROT_DIM, freqs[lane % half], 0.0).reshape(1, LANES)
    rows = ROW_TILE
    spec = pl.BlockSpec((pl.Squeezed(), rows, LANES), lambda b, i: (b, i, 0))
    shape = jax.ShapeDtypeStruct((bsz, seqlen, LANES), F32)
    return pl.pallas_call(
        _rope_kernel,
        grid=(bsz, seqlen // rows),
        in_specs=[
            spec,
            pl.BlockSpec((1, LANES), lambda b, i: (0, 0)),
        ],
        out_specs=[spec, spec, spec],
        out_shape=[shape, shape, shape],
        compiler_params=_params("arbitrary", "arbitrary"),
        name="rope_tables",
    )(jnp.broadcast_to(positions.astype(F32)[:, :, None], (bsz, seqlen, LANES)), freq_lane)


def _disc_kernel(are_ref, aim_ref, ls_ref, bre_ref, bim_ref,
                 lbre_ref, lbim_ref, bbre_ref, bbim_ref):
    delta = jnp.exp(ls_ref[...])
    lam_re = jnp.minimum(are_ref[...], -1e-4)
    lam_im = aim_ref[...]
    mag = jnp.exp(lam_re * delta)
    ang = lam_im * delta
    lb_re = mag * jnp.cos(ang)
    lb_im = mag * jnp.sin(ang)
    den = lam_re * lam_re + lam_im * lam_im
    n_re = lb_re - 1.0
    n_im = lb_im
    k_re = (n_re * lam_re + n_im * lam_im) / den
    k_im = (n_im * lam_re - n_re * lam_im) / den
    b_re = bre_ref[...]
    b_im = bim_ref[...]
    lbre_ref[...] = lb_re
    lbim_ref[...] = lb_im
    bbre_ref[...] = k_re * b_re - k_im * b_im
    bbim_ref[...] = k_re * b_im + k_im * b_re


def _discretise(a_re, a_im, log_step, b_re, b_im):
    depth, g, p = a_re.shape
    c = b_re.shape[-1]
    rows = depth * g * c

    def rep(a):
        return jnp.broadcast_to(a[:, :, None, :], (depth, g, c, p)).reshape(rows, p)

    ls = jnp.broadcast_to(log_step[:, :, None, None], (depth, g, c, 1)).reshape(rows, 1)
    bt_re = b_re.transpose(0, 1, 3, 2).reshape(rows, p)
    bt_im = b_im.transpose(0, 1, 3, 2).reshape(rows, p)
    shape = jax.ShapeDtypeStruct((rows, p), F32)
    outs = pl.pallas_call(
        _disc_kernel,
        out_shape=[shape, shape, shape, shape],
        name="s5_discretise",
    )(rep(a_re), rep(a_im), ls, bt_re, bt_im)
    lb_re, lb_im, bb_re, bb_im = [o.reshape(depth, g, c, p) for o in outs]
    return lb_re[:, :, 0, :], lb_im[:, :, 0, :], bb_re, bb_im


def _block_diag_in(w_re, w_im):
    depth, g, c, p = w_re.shape
    gh = g // 2
    w = jnp.stack([w_re, w_im], axis=1).astype(BF16).reshape(depth, 2, 2, gh, c, p)
    eye = jnp.eye(gh, dtype=BF16)
    return jnp.einsum("lqhgcp,gk->lqhgckp", w, eye).reshape(depth, 2, 2, gh * c, gh * p)


def _block_diag_out(w_re, w_im):
    depth, g, c, p = w_re.shape
    gh = g // 2
    w = jnp.stack([w_re, w_im], axis=1).astype(BF16).reshape(depth, 2, 2, gh, c, p)
    eye = jnp.eye(gh, dtype=BF16)
    return jnp.einsum("lqhgcp,gk->lqhgpkc", w, eye).reshape(depth, 2, 2, gh * p, gh * c)


def _in_proj_kernel(x_ref, mod_ref, w_ref, cos_ref, sina_ref, sinb_ref,
                    u_ref, qt_ref, k_ref, vt_ref, *, d_ssm, d_att):
    shift = mod_ref[0:1, :]
    scale = mod_ref[1:2, :]
    half = ROT_DIM // 2
    q_scale = DIFF_HEAD ** -0.5 * math.log2(math.e)
    blocks = _row_blocks((ATT_K_TILE,) * vt_ref.shape[0], x_ref.shape[0])
    projs = [_dot((x_ref[rb, :] * (1.0 + scale) + shift).astype(BF16), w_ref[...]) for rb in blocks]
    for kb, (rb, proj) in enumerate(zip(blocks, projs)):
        u_ref[rb, :] = proj[:, :d_ssm]
        cos_t = cos_ref[rb, :]
        sin_a = sina_ref[rb, :]
        sin_b = sinb_ref[rb, :]

        def rotary(t):
            return (t * cos_t + pltpu.roll(t, LANES - half, 1) * sin_a
                    + pltpu.roll(t, half, 1) * sin_b)

        for blk in range(d_att // LANES):
            cols = slice(blk * LANES, (blk + 1) * LANES)
            lo = d_ssm + blk * LANES
            qt_ref[cols, rb] = (rotary(proj[:, lo:lo + LANES]) * q_scale).T.astype(BF16)
            lo = d_ssm + d_att + blk * LANES
            k_ref[rb, cols] = rotary(proj[:, lo:lo + LANES]).astype(BF16)
            lo = d_ssm + 2 * d_att + blk * LANES
            vt_ref[kb, cols, :] = proj[:, lo:lo + LANES].T.astype(BF16)


def _in_proj(x, mod, w_in, ropes, layer, d_ssm, d_att):
    bsz, seqlen, d = x.shape
    rows = ROW_TILE
    kt = rows // ATT_K_TILE
    tok = lambda width: pl.BlockSpec((pl.Squeezed(), rows, width), lambda b, i: (b, i, 0))
    return pl.pallas_call(
        functools.partial(_in_proj_kernel, d_ssm=d_ssm, d_att=d_att),
        grid=(bsz, seqlen // rows),
        in_specs=[
            tok(d), _mod_spec(mod, layer), _layer_spec(w_in, layer),
            tok(LANES), tok(LANES), tok(LANES),
        ],
        out_specs=[
            tok(d_ssm),
            pl.BlockSpec((pl.Squeezed(), d_att, rows), lambda b, i: (b, 0, i)),
            tok(d_att),
            pl.BlockSpec((pl.Squeezed(), kt, d_att, ATT_K_TILE), lambda b, i: (b, i, 0, 0)),
        ],
        out_shape=[
            jax.ShapeDtypeStruct((bsz, seqlen, d_ssm), F32),
            jax.ShapeDtypeStruct((bsz, d_att, seqlen), BF16),
            jax.ShapeDtypeStruct((bsz, seqlen, d_att), BF16),
            jax.ShapeDtypeStruct((bsz, seqlen // ATT_K_TILE, d_att, ATT_K_TILE), BF16),
        ],
        compiler_params=_params("arbitrary", "arbitrary"),
        name="in_proj_rotary",
    )(x, mod, w_in, *ropes)


def _s5_kernel(u_ref, lam_ref, wb_ref, wc_ref, d_ref, gw_ref, gb_ref, o_ref,
               *scratch, n_slab, n_seq):
    slab_refs = scratch[:n_seq]
    carry_ref = scratch[n_seq]
    tile = u_ref.shape[1]
    pitch = SCAN_PITCH
    half_slabs = n_slab // 2
    quarter = half_slabs // 2
    skew = SUBLANES - 1

    @pl.when(pl.program_id(1) == 0)
    def _():
        carry_ref[...] = jnp.zeros_like(carry_ref)

    half_w = u_ref.shape[2] // 2
    for b in range(n_seq):
        ub = u_ref[b].astype(BF16)
        pad = jnp.zeros((SCAN_PAD, LANES), F32)
        for part in range(2):
            for gh in range(2):
                r = _dot(ub[:, gh * half_w:(gh + 1) * half_w], wb_ref[part, gh])
                for s in range(quarter):
                    base = (part * half_slabs + gh * quarter + s) * pitch
                    slab_refs[b][pl.ds(base, SCAN_PAD), :] = pad
                    slab_refs[b][pl.ds(base + SCAN_PAD, tile), :] = r[:, s * LANES:(s + 1) * LANES]
                    slab_refs[b][pl.ds(base + SCAN_PAD + tile, SCAN_PAD), :] = pad

    lam_re = [lam_ref[0, v * SUBLANES:(v + 1) * SUBLANES, :] for v in range(2)]
    lam_im = [lam_ref[1, v * SUBLANES:(v + 1) * SUBLANES, :] for v in range(2)]
    sub = lax.broadcasted_iota(jnp.int32, (SUBLANES, LANES), 0)

    def rows(n, part, v):
        first = (part * half_slabs + v * SUBLANES) * pitch
        return pl.ds(first + SCAN_PAD - skew + n, SUBLANES, stride=pitch + 1)

    def step(n, state, valid=None):
        new = []
        for b in range(n_seq):
            for v in range(2):
                x_re, x_im = state[4 * b + 2 * v], state[4 * b + 2 * v + 1]
                b_re = slab_refs[b][rows(n, 0, v), :]
                b_im = slab_refs[b][rows(n, 1, v), :]
                n_re = lam_re[v] * x_re - lam_im[v] * x_im + b_re
                n_im = lam_re[v] * x_im + lam_im[v] * x_re + b_im
                if valid is not None:
                    n_re = jnp.where(valid, n_re, x_re)
                    n_im = jnp.where(valid, n_im, x_im)
                slab_refs[b][rows(n, 0, v), :] = n_re
                slab_refs[b][rows(n, 1, v), :] = n_im
                new += [n_re, n_im]
        return tuple(new)

    state = tuple(carry_ref[b, i * SUBLANES:(i + 1) * SUBLANES, :]
                  for b in range(n_seq) for i in range(4))
    for n in range(tile + skew):
        if n < skew:
            state = step(n, state, sub >= skew - n)
        elif n < tile:
            state = step(n, state)
        else:
            state = step(n, state, sub < tile + skew - n)
    for b in range(n_seq):
        for i in range(4):
            carry_ref[b, i * SUBLANES:(i + 1) * SUBLANES, :] = state[4 * b + i]

    for b in range(n_seq):
        ys = []
        for gh in range(2):
            acc = None
            for part in range(2):
                first = part * half_slabs + gh * quarter
                xs = jnp.concatenate(
                    [slab_refs[b][pl.ds((first + s) * pitch + SCAN_PAD, tile), :]
                     for s in range(quarter)], axis=1)
                term = _dot(xs.astype(BF16), wc_ref[part, gh])
                acc = term if part == 0 else acc - term
            ys.append(acc)
        y = jnp.concatenate(ys, axis=1) + d_ref[...] * u_ref[b]
        y = _gelu_tanh(y)
        gate = _dot(y.astype(BF16), gw_ref[...]) + gb_ref[...]
        o_ref[b] = (y * _sigmoid(gate)).astype(o_ref.dtype)


def _s5_mixer(u, lam, wb, wc, d_skip, glu_w, glu_b, layer):
    bsz, seqlen, d_ssm = u.shape
    tile = SCAN_TILE
    n_seq = SCAN_BATCH
    n_slab = 2 * lam.shape[2]
    assert bsz % n_seq == 0 and seqlen % tile == 0
    full = lambda a: _layer_spec(a, layer)
    return pl.pallas_call(
        functools.partial(_s5_kernel, n_slab=n_slab, n_seq=n_seq),
        grid=(bsz // n_seq, seqlen // tile),
        in_specs=[
            pl.BlockSpec((n_seq, tile, d_ssm), lambda b, i: (b, i, 0)),
            full(lam), full(wb), full(wc), full(d_skip), full(glu_w), full(glu_b),
        ],
        out_specs=pl.BlockSpec((n_seq, tile, d_ssm), lambda b, i: (b, i, 0)),
        out_shape=jax.ShapeDtypeStruct((bsz, seqlen, d_ssm), BF16),
        scratch_shapes=[pltpu.VMEM((n_slab * SCAN_PITCH, LANES), F32) for _ in range(n_seq)]
                       + [pltpu.VMEM((n_seq, n_slab, LANES), F32)],
        compiler_params=_params("arbitrary", "arbitrary"),
        name="s5_mixer",
    )(u, lam, wb, wc, d_skip, glu_w, glu_b)


def _attn_kernel(qt_ref, k_ref, vt_ref, lq1_ref, lk1_ref, lq2_ref, lk2_ref, sub_ref, o_ref,
                 p_ref, acc_ref, *, lam_init, heads, seqs):
    tq = qt_ref.shape[2]
    tk = ATT_K_TILE
    head_w = 2 * DIFF_HEAD
    qi = pl.program_id(2)
    chains = [(b, h) for b in range(seqs) for h in range(heads)]
    ones = jnp.ones((BF16_ROWS, tk), BF16)

    rhs = []
    for b, h in chains:
        qt = qt_ref[b, h * head_w:(h + 1) * head_w, :]
        sub = lax.broadcasted_iota(jnp.int32, qt.shape, 0)
        zero = jnp.zeros_like(qt)
        rhs.append(jnp.concatenate([jnp.where(sub < DIFF_HEAD, qt, zero),
                                    jnp.where(sub >= DIFF_HEAD, qt, zero)], axis=1))

    def scores(j, c):
        b, h = chains[c]
        start = pl.multiple_of(j * tk, tk)
        return _dot(k_ref[b, pl.ds(start, tk), h * head_w:(h + 1) * head_w], rhs[c])

    def pv(j, c):
        b, h = chains[c]
        lhs = jnp.concatenate([vt_ref[b, j, h * head_w:(h + 1) * head_w, :], ones], axis=0)
        return _dot(lhs, p_ref[c])

    stats = []
    for c in range(len(chains)):
        s = scores(qi, c)
        key = lax.broadcasted_iota(jnp.int32, s.shape, 0)
        qry = lax.broadcasted_iota(jnp.int32, s.shape, 1) % tq
        s = jnp.where(key <= qry, s, -jnp.inf)
        m = jnp.max(s, axis=0, keepdims=True)
        p_ref[c] = jnp.exp2(s - m).astype(BF16)
        acc_ref[c] = jnp.zeros(acc_ref.shape[1:], F32)
        stats.append((m, jnp.ones_like(m)))

    def step(j, pending, carry):
        out = []
        for c in range(len(chains)):
            m, alpha_p = carry[c]
            acc_ref[c] = alpha_p * acc_ref[c] + pv(pending, c)
            s = scores(j, c)
            m_new = jnp.maximum(m, jnp.max(s, axis=0, keepdims=True))
            p_ref[c] = jnp.exp2(s - m_new).astype(BF16)
            out.append((m_new, jnp.exp2(m - m_new)))
        return tuple(out)

    def pair(i, carry):
        j = 2 * i
        carry = step(j, jnp.where(i == 0, qi, j - 1), carry)
        return step(j + 1, j, carry)

    carry = lax.fori_loop(0, qi // 2, pair, tuple(stats))
    last = qi - 1
    carry = lax.fori_loop(
        0, qi % 2, lambda _, c: step(last, jnp.where(last == 0, qi, last - 1), c), carry)
    pending = jnp.where(qi == 0, qi, last)

    lam = (jnp.exp(jnp.sum(lq1_ref[...] * lk1_ref[...], axis=-1, keepdims=True))
           - jnp.exp(jnp.sum(lq2_ref[...] * lk2_ref[...], axis=-1, keepdims=True)) + lam_init)
    gain = sub_ref[...] * (1.0 - lam_init)
    for c, (b, h) in enumerate(chains):
        _, alpha_p = carry[c]
        acc = alpha_p * acc_ref[c] + pv(pending, c)
        inv = 1.0 / acc[head_w:head_w + 1, :]
        o = acc[:head_w, :tq] * inv[:, :tq] - acc[:head_w, tq:] * (lam * inv[:, tq:])
        o = o * lax.rsqrt(jnp.mean(o * o, axis=0, keepdims=True) + RMS_EPS) * gain
        o_ref[b, :, h * head_w:(h + 1) * head_w] = o.T.astype(o_ref.dtype)


def _diff_attention(qt, k, vt, lq1, lk1, lq2, lk2, subln_col, layer, lam_init):
    bsz, seqlen, d_att = k.shape
    head_w = 2 * DIFF_HEAD
    heads = ATT_HEADS_PER_STEP
    seqs = ATT_SEQS_PER_STEP
    width = heads * head_w
    assert ATT_Q_TILE == ATT_K_TILE and d_att % width == 0 and bsz % seqs == 0
    small = lambda a: _layer_spec(a, layer)
    return pl.pallas_call(
        functools.partial(_attn_kernel, lam_init=lam_init, heads=heads, seqs=seqs),
        grid=(bsz // seqs, d_att // width, seqlen // ATT_Q_TILE),
        in_specs=[
            pl.BlockSpec((seqs, width, ATT_Q_TILE), lambda b, h, i: (b, h, i)),
            pl.BlockSpec((seqs, seqlen, width), lambda b, h, i: (b, 0, h)),
            pl.BlockSpec((seqs, seqlen // ATT_K_TILE, width, ATT_K_TILE),
                         lambda b, h, i: (b, 0, h, 0)),
            small(lq1), small(lk1), small(lq2), small(lk2), small(subln_col),
        ],
        out_specs=pl.BlockSpec((seqs, ATT_Q_TILE, width), lambda b, h, i: (b, i, h)),
        out_shape=jax.ShapeDtypeStruct((bsz, seqlen, d_att), BF16),
        scratch_shapes=[
            pltpu.VMEM((seqs * heads, ATT_K_TILE, 2 * ATT_Q_TILE), BF16),
            pltpu.VMEM((seqs * heads, head_w + BF16_ROWS, 2 * ATT_Q_TILE), F32),
        ],
        compiler_params=_params("arbitrary", "arbitrary", "arbitrary"),
        name="diff_attention",
    )(qt, k, vt, lq1, lk1, lq2, lk2, subln_col)


def _mix_ffn_kernel(x_ref, ssm_ref, att_ref, mod_ref, wo_ref, g1_ref, b1_ref,
                    wg_ref, wu_ref, wd_ref, g2_ref, b2_ref, o_ref, *, alpha, d_ssm):
    gate1 = mod_ref[2:3, :]
    shift2 = mod_ref[3:4, :]
    scale2 = mod_ref[4:5, :]
    gate2 = mod_ref[5:6, :]
    blocks = _row_blocks(FFN_ROW_SPLIT, x_ref.shape[0])
    mix = [_dot(ssm_ref[rb, :], wo_ref[:d_ssm, :]) + _dot(att_ref[rb, :], wo_ref[d_ssm:, :])
           for rb in blocks]
    x1 = [_layer_norm(alpha * x_ref[rb, :] + (1.0 + gate1) * m, g1_ref[...], b1_ref[...])
          for rb, m in zip(blocks, mix)]
    h = [(v * (1.0 + scale2) + shift2).astype(BF16) for v in x1]
    a = [(_silu(_dot(v, wg_ref[...])) * _dot(v, wu_ref[...])).astype(BF16) for v in h]
    ffn = [_dot(v, wd_ref[...]) for v in a]
    for rb, v, f in zip(blocks, x1, ffn):
        o_ref[rb, :] = _layer_norm(alpha * v + (1.0 + gate2) * f, g2_ref[...], b2_ref[...])


def _mix_ffn(x, ssm, att, mod, w_out, g1, b1, wg, wu, wd, g2, b2, layer, alpha):
    bsz, seqlen, d = x.shape
    rows = ROW_TILE
    d_ssm = ssm.shape[-1]
    tok = lambda width: pl.BlockSpec((pl.Squeezed(), rows, width), lambda bb, i: (bb, i, 0))
    resident = lambda a: _layer_spec(a, layer)
    return pl.pallas_call(
        functools.partial(_mix_ffn_kernel, alpha=alpha, d_ssm=d_ssm),
        grid=(bsz, seqlen // rows),
        in_specs=[
            tok(d), tok(d_ssm), tok(att.shape[-1]), _mod_spec(mod, layer),
            resident(w_out), resident(g1), resident(b1),
            resident(wg), resident(wu), resident(wd), resident(g2), resident(b2),
        ],
        out_specs=tok(d),
        out_shape=jax.ShapeDtypeStruct(x.shape, F32),
        compiler_params=_params("arbitrary", "arbitrary"),
        name="out_proj_swiglu_ln",
    )(x, ssm, att, mod, w_out, g1, b1, wg, wu, wd, g2, b2)


def kernel(x, c, positions, mod_w, mod_b, w_in, ssm_a_re, ssm_a_im, ssm_log_step, ssm_b_re, ssm_b_im, ssm_c_re, ssm_c_im, ssm_d, glu_w, glu_b, lam_q1, lam_k1, lam_q2, lam_k2, subln_w, w_out, ln1_g, ln1_b, ffn_w_gate, ffn_w_up, ffn_w_down, ln2_g, ln2_b):
    bsz, seqlen, d = x.shape
    depth = mod_w.shape[0]
    n_groups = ssm_a_re.shape[1]
    d_ssm = n_groups * SSM_GROUP
    d_att = (w_in.shape[2] - d_ssm) // 3
    alpha = (2 * depth) ** 0.25

    mod = _modulation(c, mod_w, mod_b).reshape(depth, bsz, 6, d)
    ropes = _rope_tables(positions)
    lb_re, lb_im, bb_re, bb_im = _discretise(ssm_a_re, ssm_a_im, ssm_log_step, ssm_b_re, ssm_b_im)
    n_lane_rows = n_groups * SSM_STATE // LANES
    lam = jnp.stack([lb_re.reshape(depth, n_lane_rows, LANES),
                     lb_im.reshape(depth, n_lane_rows, LANES)], axis=1)
    wb = _block_diag_in(bb_re, bb_im)
    wc = _block_diag_out(ssm_c_re, ssm_c_im)
    d_skip = ssm_d.reshape(depth, 1, d_ssm)
    row = lambda a: a.reshape(depth, 1, a.shape[-1])
    subln_col = subln_w.reshape(depth, -1, 1)

    w_in_b = w_in.astype(BF16)
    glu_w_b = glu_w.astype(BF16)
    w_out_b = w_out.astype(BF16)
    wg_b = ffn_w_gate.astype(BF16)
    wu_b = ffn_w_up.astype(BF16)
    wd_b = ffn_w_down.astype(BF16)
    glu_b_r, lq1, lk1, lq2, lk2 = row(glu_b), row(lam_q1), row(lam_k1), row(lam_q2), row(lam_k2)
    g1, b1, g2, b2 = row(ln1_g), row(ln1_b), row(ln2_g), row(ln2_b)

    for l in range(depth):
        lam_init = 0.8 - 0.6 * math.exp(-0.3 * l)
        u, qt, k, vt = _in_proj(x, mod, w_in_b, ropes, l, d_ssm, d_att)
        ssm = _s5_mixer(u, lam, wb, wc, d_skip, glu_w_b, glu_b_r, l)
        att = _diff_attention(qt, k, vt, lq1, lk1, lq2, lk2, subln_col, l, lam_init)
        x = _mix_ffn(x, ssm, att, mod, w_out_b, g1, b1, wg_b, wu_b, wd_b, g2, b2, l, alpha)
    return x
```

```python
import functools
import math

import jax
import jax.numpy as jnp
from jax import lax
from jax.experimental import pallas as pl
from jax.experimental.pallas import tpu as pltpu

F32 = jnp.float32
BF16 = jnp.bfloat16

SSM_GROUP = 16
SSM_STATE = 64
DIFF_HEAD = 64
ROT_DIM = DIFF_HEAD // 4
ROPE_THETA = 500000.0
LN_EPS = 1e-5
RMS_EPS = 1e-5

LANES = 128
SUBLANES = 8
BF16_ROWS = 16
VMEM_LIMIT_BYTES = 56 * 1024 * 1024

ROW_TILE = 512
FFN_ROW_SPLIT = (192, 192, 128)
SCAN_TILE = 256
SCAN_PAD = SUBLANES
SCAN_PITCH = SCAN_TILE + 2 * SCAN_PAD
SCAN_BATCH = 2
ATT_Q_TILE = 256
ATT_K_TILE = 256
ATT_HEADS_PER_STEP = 4
ATT_SEQS_PER_STEP = 4


def _sigmoid(x):
    return 1.0 / (1.0 + jnp.exp(-x))


def _silu(x):
    return x * _sigmoid(x)


def _gelu_tanh(x):
    c = math.sqrt(2.0 / math.pi)
    return 0.5 * x * (1.0 + jnp.tanh(c * (x + 0.044715 * (x * x * x))))


def _dot(a, b):
    return jnp.dot(a, b, preferred_element_type=F32)


def _layer_norm(y, g, b):
    mu = jnp.mean(y, axis=-1, keepdims=True)
    yc = y - mu
    var = jnp.mean(yc * yc, axis=-1, keepdims=True)
    return yc * lax.rsqrt(var + LN_EPS) * g + b


def _row_blocks(sizes, total):
    assert sum(sizes) == total
    starts = [sum(sizes[:i]) for i in range(len(sizes))]
    return [slice(s, s + n) for s, n in zip(starts, sizes)]


def _layer_spec(a, layer):
    zeros = (0,) * (a.ndim - 1)
    return pl.BlockSpec((pl.Squeezed(),) + a.shape[1:], lambda *_: (layer,) + zeros,
                        pipeline_mode=pl.Buffered(1))


def _mod_spec(mod, layer):
    return pl.BlockSpec((pl.Squeezed(), pl.Squeezed()) + mod.shape[2:],
                        lambda b, *_: (layer, b, 0, 0))


def _params(*sem):
    return pltpu.CompilerParams(dimension_semantics=sem, vmem_limit_bytes=VMEM_LIMIT_BYTES)


def _mod_kernel(c_ref, w_ref, b_ref, o_ref):
    cond = _silu(c_ref[...])
    o_ref[...] = _dot(cond.astype(BF16), w_ref[...].astype(BF16)) + b_ref[...]


def _modulation(c, mod_w, mod_b):
    depth, d, six_d = mod_w.shape
    bsz = c.shape[0]
    n_col = six_d // d
    return pl.pallas_call(
        _mod_kernel,
        grid=(depth, n_col),
        in_specs=[
            pl.BlockSpec((bsz, d), lambda l, j: (0, 0)),
            pl.BlockSpec((pl.Squeezed(), d, d), lambda l, j: (l, 0, j)),
            pl.BlockSpec((pl.Squeezed(), 1, d), lambda l, j: (l, 0, j)),
        ],
        out_specs=pl.BlockSpec((pl.Squeezed(), bsz, d), lambda l, j: (l, 0, j)),
        out_shape=jax.ShapeDtypeStruct((depth, bsz, six_d), F32),
        compiler_params=_params("arbitrary", "arbitrary"),
        name="adaln_modulation",
    )(c, mod_w, mod_b.reshape(depth, 1, six_d))


def _rope_kernel(pos_ref, freq_ref, cos_ref, sin_ref, nsin_ref):
    ang = pos_ref[...] * freq_ref[...]
    s = jnp.sin(ang)
    cos_ref[...] = jnp.cos(ang)
    sin_ref[...] = s
    nsin_ref[...] = -s


def _rope_tables(positions):
    bsz, seqlen = positions.shape
    half = ROT_DIM // 2
    per_row = LANES // half
    n_rows = bsz * seqlen // per_row
    rows = min(ROW_TILE, n_rows)
    assert n_rows % rows == 0
    freqs = ROPE_THETA ** (-jnp.arange(0, ROT_DIM, 2, dtype=F32) / ROT_DIM)
    pos = jnp.repeat(positions.astype(F32).reshape(n_rows, per_row), half, axis=1)
    spec = pl.BlockSpec((rows, LANES), lambda i: (i, 0))
    shape = jax.ShapeDtypeStruct((n_rows, LANES), F32)
    cos, sin, nsin = pl.pallas_call(
        _rope_kernel,
        grid=(n_rows // rows,),
        in_specs=[spec, pl.BlockSpec((1, LANES), lambda i: (0, 0))],
        out_specs=[spec, spec, spec],
        out_shape=[shape, shape, shape],
        compiler_params=_params("arbitrary"),
        name="rope_tables",
    )(pos, jnp.tile(freqs, per_row).reshape(1, LANES))
    cos, sin, nsin = [t.reshape(bsz, seqlen, half) for t in (cos, sin, nsin)]
    one = jnp.ones((bsz, seqlen, DIFF_HEAD - ROT_DIM), F32)
    zero = jnp.zeros((bsz, seqlen, DIFF_HEAD - half), F32)
    period = lambda parts: jnp.concatenate(parts * (LANES // DIFF_HEAD), axis=-1)
    cos_t = period([cos, cos, one])
    sin_a = period([nsin, zero])
    sin_b = period([zero[..., :half], sin, zero[..., :DIFF_HEAD - ROT_DIM]])
    return cos_t, sin_a, sin_b


def _disc_kernel(are_ref, aim_ref, ls_ref, bre_ref, bim_ref,
                 lbre_ref, lbim_ref, bbre_ref, bbim_ref):
    delta = jnp.exp(ls_ref[...])
    lam_re = jnp.minimum(are_ref[...], -1e-4)
    lam_im = aim_ref[...]
    mag = jnp.exp(lam_re * delta)
    ang = lam_im * delta
    lb_re = mag * jnp.cos(ang)
    lb_im = mag * jnp.sin(ang)
    den = lam_re * lam_re + lam_im * lam_im
    n_re = lb_re - 1.0
    n_im = lb_im
    k_re = (n_re * lam_re + n_im * lam_im) / den
    k_im = (n_im * lam_re - n_re * lam_im) / den
    b_re = bre_ref[...]
    b_im = bim_ref[...]
    lbre_ref[...] = lb_re
    lbim_ref[...] = lb_im
    bbre_ref[...] = k_re * b_re - k_im * b_im
    bbim_ref[...] = k_re * b_im + k_im * b_re


def _discretise(a_re, a_im, log_step, b_re, b_im):
    depth, g, p = a_re.shape
    c = b_re.shape[-1]
    rows = depth * g * c

    def rep(a):
        return jnp.broadcast_to(a[:, :, None, :], (depth, g, c, p)).reshape(rows, p)

    ls = jnp.broadcast_to(log_step[:, :, None, None], (depth, g, c, 1)).reshape(rows, 1)
    bt_re = b_re.transpose(0, 1, 3, 2).reshape(rows, p)
    bt_im = b_im.transpose(0, 1, 3, 2).reshape(rows, p)
    shape = jax.ShapeDtypeStruct((rows, p), F32)
    outs = pl.pallas_call(
        _disc_kernel,
        out_shape=[shape, shape, shape, shape],
        name="s5_discretise",
    )(rep(a_re), rep(a_im), ls, bt_re, bt_im)
    lb_re, lb_im, bb_re, bb_im = [o.reshape(depth, g, c, p) for o in outs]
    return lb_re[:, :, 0, :], lb_im[:, :, 0, :], bb_re, bb_im


def _block_diag_in(w_re, w_im):
    depth, g, c, p = w_re.shape
    gh = g // 2
    w = jnp.stack([w_re, w_im], axis=1).astype(BF16).reshape(depth, 2, 2, gh, c, p)
    eye = jnp.eye(gh, dtype=BF16)
    return jnp.einsum("lqhgcp,gk->lqhgckp", w, eye).reshape(depth, 2, 2, gh * c, gh * p)


def _block_diag_out(w_re, w_im):
    depth, g, c, p = w_re.shape
    gh = g // 2
    w = jnp.stack([w_re, w_im], axis=1).astype(BF16).reshape(depth, 2, 2, gh, c, p)
    eye = jnp.eye(gh, dtype=BF16)
    return jnp.einsum("lqhgcp,gk->lqhgpkc", w, eye).reshape(depth, 2, 2, gh * p, gh * c)


def _in_proj_kernel(x_ref, mod_ref, w_ref, cos_ref, sina_ref, sinb_ref,
                    u_ref, qt_ref, k_ref, vt_ref, *, d_ssm, d_att):
    shift = mod_ref[0:1, :]
    scale = mod_ref[1:2, :]
    half = ROT_DIM // 2
    q_scale = DIFF_HEAD ** -0.5 * math.log2(math.e)
    blocks = _row_blocks((ATT_K_TILE,) * vt_ref.shape[0], x_ref.shape[0])
    projs = [_dot((x_ref[rb, :] * (1.0 + scale) + shift).astype(BF16), w_ref[...]) for rb in blocks]
    for kb, (rb, proj) in enumerate(zip(blocks, projs)):
        u_ref[rb, :] = proj[:, :d_ssm]
        cos_t = cos_ref[rb, :]
        sin_a = sina_ref[rb, :]
        sin_b = sinb_ref[rb, :]

        def rotary(t):
            return (t * cos_t + pltpu.roll(t, LANES - half, 1) * sin_a
                    + pltpu.roll(t, half, 1) * sin_b)

        for blk in range(d_att // LANES):
            cols = slice(blk * LANES, (blk + 1) * LANES)
            lo = d_ssm + blk * LANES
            qt_ref[cols, rb] = (rotary(proj[:, lo:lo + LANES]) * q_scale).T.astype(BF16)
            lo = d_ssm + d_att + blk * LANES
            k_ref[rb, cols] = rotary(proj[:, lo:lo + LANES]).astype(BF16)
            lo = d_ssm + 2 * d_att + blk * LANES
            vt_ref[kb, cols, :] = proj[:, lo:lo + LANES].T.astype(BF16)


def _in_proj(x, mod, w_in, ropes, layer, d_ssm, d_att):
    bsz, seqlen, d = x.shape
    rows = ROW_TILE
    kt = rows // ATT_K_TILE
    tok = lambda width: pl.BlockSpec((pl.Squeezed(), rows, width), lambda b, i: (b, i, 0))
    return pl.pallas_call(
        functools.partial(_in_proj_kernel, d_ssm=d_ssm, d_att=d_att),
        grid=(bsz, seqlen // rows),
        in_specs=[
            tok(d), _mod_spec(mod, layer), _layer_spec(w_in, layer),
            tok(LANES), tok(LANES), tok(LANES),
        ],
        out_specs=[
            tok(d_ssm),
            pl.BlockSpec((pl.Squeezed(), d_att, rows), lambda b, i: (b, 0, i)),
            tok(d_att),
            pl.BlockSpec((pl.Squeezed(), kt, d_att, ATT_K_TILE), lambda b, i: (b, i, 0, 0)),
        ],
        out_shape=[
            jax.ShapeDtypeStruct((bsz, seqlen, d_ssm), F32),
            jax.ShapeDtypeStruct((bsz, d_att, seqlen), BF16),
            jax.ShapeDtypeStruct((bsz, seqlen, d_att), BF16),
            jax.ShapeDtypeStruct((bsz, seqlen // ATT_K_TILE, d_att, ATT_K_TILE), BF16),
        ],
        compiler_params=_params("arbitrary", "arbitrary"),
        name="in_proj_rotary",
    )(x, mod, w_in, *ropes)


def _s5_kernel(u_ref, lam_ref, wb_ref, wc_ref, d_ref, gw_ref, gb_ref, o_ref,
               *scratch, n_slab, n_seq):
    slab_refs = scratch[:n_seq]
    carry_ref = scratch[n_seq]
    tile = u_ref.shape[1]
    pitch = SCAN_PITCH
    half_slabs = n_slab // 2
    quarter = half_slabs // 2
    skew = SUBLANES - 1

    @pl.when(pl.program_id(1) == 0)
    def _():
        carry_ref[...] = jnp.zeros_like(carry_ref)

    half_w = u_ref.shape[2] // 2
    for b in range(n_seq):
        ub = u_ref[b].astype(BF16)
        pad = jnp.zeros((SCAN_PAD, LANES), F32)
        for part in range(2):
            for gh in range(2):
                r = _dot(ub[:, gh * half_w:(gh + 1) * half_w], wb_ref[part, gh])
                for s in range(quarter):
                    base = (part * half_slabs + gh * quarter + s) * pitch
                    slab_refs[b][pl.ds(base, SCAN_PAD), :] = pad
                    slab_refs[b][pl.ds(base + SCAN_PAD, tile), :] = r[:, s * LANES:(s + 1) * LANES]
                    slab_refs[b][pl.ds(base + SCAN_PAD + tile, SCAN_PAD), :] = pad

    lam_re = [lam_ref[0, v * SUBLANES:(v + 1) * SUBLANES, :] for v in range(2)]
    lam_im = [lam_ref[1, v * SUBLANES:(v + 1) * SUBLANES, :] for v in range(2)]
    sub = lax.broadcasted_iota(jnp.int32, (SUBLANES, LANES), 0)

    def rows(n, part, v):
        first = (part * half_slabs + v * SUBLANES) * pitch
        return pl.ds(first + SCAN_PAD - skew + n, SUBLANES, stride=pitch + 1)

    def step(n, state, valid=None):
        new = []
        for b in range(n_seq):
            for v in range(2):
                x_re, x_im = state[4 * b + 2 * v], state[4 * b + 2 * v + 1]
                b_re = slab_refs[b][rows(n, 0, v), :]
                b_im = slab_refs[b][rows(n, 1, v), :]
                n_re = lam_re[v] * x_re - lam_im[v] * x_im + b_re
                n_im = lam_re[v] * x_im + lam_im[v] * x_re + b_im
                if valid is not None:
                    n_re = jnp.where(valid, n_re, x_re)
                    n_im = jnp.where(valid, n_im, x_im)
                slab_refs[b][rows(n, 0, v), :] = n_re
                slab_refs[b][rows(n, 1, v), :] = n_im
                new += [n_re, n_im]
        return tuple(new)

    state = tuple(carry_ref[b, i * SUBLANES:(i + 1) * SUBLANES, :]
                  for b in range(n_seq) for i in range(4))
    for n in range(tile + skew):
        if n < skew:
            state = step(n, state, sub >= skew - n)
        elif n < tile:
            state = step(n, state)
        else:
            state = step(n, state, sub < tile + skew - n)
    for b in range(n_seq):
        for i in range(4):
            carry_ref[b, i * SUBLANES:(i + 1) * SUBLANES, :] = state[4 * b + i]

    for b in range(n_seq):
        ys = []
        for gh in range(2):
            acc = None
            for part in range(2):
                first = part * half_slabs + gh * quarter
                xs = jnp.concatenate(
                    [slab_refs[b][pl.ds((first + s) * pitch + SCAN_PAD, tile), :]
                     for s in range(quarter)], axis=1)
                term = _dot(xs.astype(BF16), wc_ref[part, gh])
                acc = term if part == 0 else acc - term
            ys.append(acc)
        y = jnp.concatenate(ys, axis=1) + d_ref[...] * u_ref[b]
        y = _gelu_tanh(y)
        gate = _dot(y.astype(BF16), gw_ref[...]) + gb_ref[...]
        o_ref[b] = (y * _sigmoid(gate)).astype(o_ref.dtype)


def _s5_mixer(u, lam, wb, wc, d_skip, glu_w, glu_b, layer):
    bsz, seqlen, d_ssm = u.shape
    tile = SCAN_TILE
    n_seq = SCAN_BATCH
    n_slab = 2 * lam.shape[2]
    assert bsz % n_seq == 0 and seqlen % tile == 0
    full = lambda a: _layer_spec(a, layer)
    return pl.pallas_call(
        functools.partial(_s5_kernel, n_slab=n_slab, n_seq=n_seq),
        grid=(bsz // n_seq, seqlen // tile),
        in_specs=[
            pl.BlockSpec((n_seq, tile, d_ssm), lambda b, i: (b, i, 0)),
            full(lam), full(wb), full(wc), full(d_skip), full(glu_w), full(glu_b),
        ],
        out_specs=pl.BlockSpec((n_seq, tile, d_ssm), lambda b, i: (b, i, 0)),
        out_shape=jax.ShapeDtypeStruct((bsz, seqlen, d_ssm), BF16),
        scratch_shapes=[pltpu.VMEM((n_slab * SCAN_PITCH, LANES), F32) for _ in range(n_seq)]
                       + [pltpu.VMEM((n_seq, n_slab, LANES), F32)],
        compiler_params=_params("arbitrary", "arbitrary"),
        name="s5_mixer",
    )(u, lam, wb, wc, d_skip, glu_w, glu_b)


def _attn_kernel(qt_ref, k_ref, vt_ref, lq1_ref, lk1_ref, lq2_ref, lk2_ref, sub_ref, o_ref,
                 p_ref, acc_ref, *, lam_init, heads, seqs):
    tq = qt_ref.shape[2]
    tk = ATT_K_TILE
    head_w = 2 * DIFF_HEAD
    qi = pl.program_id(2)
    chains = [(b, h) for b in range(seqs) for h in range(heads)]
    ones = jnp.ones((BF16_ROWS, tk), BF16)

    rhs = []
    for b, h in chains:
        qt = qt_ref[b, h * head_w:(h + 1) * head_w, :]
        sub = lax.broadcasted_iota(jnp.int32, qt.shape, 0)
        zero = jnp.zeros_like(qt)
        rhs.append(jnp.concatenate([jnp.where(sub < DIFF_HEAD, qt, zero),
                                    jnp.where(sub >= DIFF_HEAD, qt, zero)], axis=1))

    def scores(j, c):
        b, h = chains[c]
        start = pl.multiple_of(j * tk, tk)
        return _dot(k_ref[b, pl.ds(start, tk), h * head_w:(h + 1) * head_w], rhs[c])

    def pv(j, c):
        b, h = chains[c]
        lhs = jnp.concatenate([vt_ref[b, j, h * head_w:(h + 1) * head_w, :], ones], axis=0)
        return _dot(lhs, p_ref[c])

    stats = []
    for c in range(len(chains)):
        s = scores(qi, c)
        key = lax.broadcasted_iota(jnp.int32, s.shape, 0)
        qry = lax.broadcasted_iota(jnp.int32, s.shape, 1) % tq
        s = jnp.where(key <= qry, s, -jnp.inf)
        m = jnp.max(s, axis=0, keepdims=True)
        p_ref[c] = jnp.exp2(s - m).astype(BF16)
        acc_ref[c] = jnp.zeros(acc_ref.shape[1:], F32)
        stats.append((m, jnp.ones_like(m)))

    def step(j, pending, carry):
        out = []
        for c in range(len(chains)):
            m, alpha_p = carry[c]
            acc_ref[c] = alpha_p * acc_ref[c] + pv(pending, c)
            s = scores(j, c)
            m_new = jnp.maximum(m, jnp.max(s, axis=0, keepdims=True))
            p_ref[c] = jnp.exp2(s - m_new).astype(BF16)
            out.append((m_new, jnp.exp2(m - m_new)))
        return tuple(out)

    def pair(i, carry):
        j = 2 * i
        carry = step(j, jnp.where(i == 0, qi, j - 1), carry)
        return step(j + 1, j, carry)

    carry = lax.fori_loop(0, qi // 2, pair, tuple(stats))
    last = qi - 1
    carry = lax.fori_loop(
        0, qi % 2, lambda _, c: step(last, jnp.where(last == 0, qi, last - 1), c), carry)
    pending = jnp.where(qi == 0, qi, last)

    lam = (jnp.exp(jnp.sum(lq1_ref[...] * lk1_ref[...], axis=-1, keepdims=True))
           - jnp.exp(jnp.sum(lq2_ref[...] * lk2_ref[...], axis=-1, keepdims=True)) + lam_init)
    gain = sub_ref[...] * (1.0 - lam_init)
    for c, (b, h) in enumerate(chains):
        _, alpha_p = carry[c]
        acc = alpha_p * acc_ref[c] + pv(pending, c)
        inv = 1.0 / acc[head_w:head_w + 1, :]
        o = acc[:head_w, :tq] * inv[:, :tq] - acc[:head_w, tq:] * (lam * inv[:, tq:])
        o = o * lax.rsqrt(jnp.mean(o * o, axis=0, keepdims=True) + RMS_EPS) * gain
        o_ref[b, :, h * head_w:(h + 1) * head_w] = o.T.astype(o_ref.dtype)


def _diff_attention(qt, k, vt, lq1, lk1, lq2, lk2, subln_col, layer, lam_init):
    bsz, seqlen, d_att = k.shape
    head_w = 2 * DIFF_HEAD
    heads = ATT_HEADS_PER_STEP
    seqs = ATT_SEQS_PER_STEP
    width = heads * head_w
    assert ATT_Q_TILE == ATT_K_TILE and d_att % width == 0 and bsz % seqs == 0
    small = lambda a: _layer_spec(a, layer)
    return pl.pallas_call(
        functools.partial(_attn_kernel, lam_init=lam_init, heads=heads, seqs=seqs),
        grid=(bsz // seqs, d_att // width, seqlen // ATT_Q_TILE),
        in_specs=[
            pl.BlockSpec((seqs, width, ATT_Q_TILE), lambda b, h, i: (b, h, i)),
            pl.BlockSpec((seqs, seqlen, width), lambda b, h, i: (b, 0, h)),
            pl.BlockSpec((seqs, seqlen // ATT_K_TILE, width, ATT_K_TILE),
                         lambda b, h, i: (b, 0, h, 0)),
            small(lq1), small(lk1), small(lq2), small(lk2), small(subln_col),
        ],
        out_specs=pl.BlockSpec((seqs, ATT_Q_TILE, width), lambda b, h, i: (b, i, h)),
        out_shape=jax.ShapeDtypeStruct((bsz, seqlen, d_att), BF16),
        scratch_shapes=[
            pltpu.VMEM((seqs * heads, ATT_K_TILE, 2 * ATT_Q_TILE), BF16),
            pltpu.VMEM((seqs * heads, head_w + BF16_ROWS, 2 * ATT_Q_TILE), F32),
        ],
        compiler_params=_params("arbitrary", "arbitrary", "arbitrary"),
        name="diff_attention",
    )(qt, k, vt, lq1, lk1, lq2, lk2, subln_col)


def _mix_ffn_kernel(x_ref, ssm_ref, att_ref, mod_ref, wo_ref, g1_ref, b1_ref,
                    wg_ref, wu_ref, wd_ref, g2_ref, b2_ref, o_ref, *, alpha, d_ssm):
    gate1 = mod_ref[2:3, :]
    shift2 = mod_ref[3:4, :]
    scale2 = mod_ref[4:5, :]
    gate2 = mod_ref[5:6, :]
    blocks = _row_blocks(FFN_ROW_SPLIT, x_ref.shape[0])
    mix = [_dot(ssm_ref[rb, :], wo_ref[:d_ssm, :]) + _dot(att_ref[rb, :], wo_ref[d_ssm:, :])
           for rb in blocks]
    x1 = [_layer_norm(alpha * x_ref[rb, :] + (1.0 + gate1) * m, g1_ref[...], b1_ref[...])
          for rb, m in zip(blocks, mix)]
    h = [(v * (1.0 + scale2) + shift2).astype(BF16) for v in x1]
    a = [(_silu(_dot(v, wg_ref[...])) * _dot(v, wu_ref[...])).astype(BF16) for v in h]
    ffn = [_dot(v, wd_ref[...]) for v in a]
    for rb, v, f in zip(blocks, x1, ffn):
        o_ref[rb, :] = _layer_norm(alpha * v + (1.0 + gate2) * f, g2_ref[...], b2_ref[...])


def _mix_ffn(x, ssm, att, mod, w_out, g1, b1, wg, wu, wd, g2, b2, layer, alpha):
    bsz, seqlen, d = x.shape
    rows = ROW_TILE
    d_ssm = ssm.shape[-1]
    tok = lambda width: pl.BlockSpec((pl.Squeezed(), rows, width), lambda bb, i: (bb, i, 0))
    resident = lambda a: _layer_spec(a, layer)
    return pl.pallas_call(
        functools.partial(_mix_ffn_kernel, alpha=alpha, d_ssm=d_ssm),
        grid=(bsz, seqlen // rows),
        in_specs=[
            tok(d), tok(d_ssm), tok(att.shape[-1]), _mod_spec(mod, layer),
            resident(w_out), resident(g1), resident(b1),
            resident(wg), resident(wu), resident(wd), resident(g2), resident(b2),
        ],
        out_specs=tok(d),
        out_shape=jax.ShapeDtypeStruct(x.shape, F32),
        compiler_params=_params("arbitrary", "arbitrary"),
        name="out_proj_swiglu_ln",
    )(x, ssm, att, mod, w_out, g1, b1, wg, wu, wd, g2, b2)


def kernel(x, c, positions, mod_w, mod_b, w_in, ssm_a_re, ssm_a_im, ssm_log_step, ssm_b_re, ssm_b_im, ssm_c_re, ssm_c_im, ssm_d, glu_w, glu_b, lam_q1, lam_k1, lam_q2, lam_k2, subln_w, w_out, ln1_g, ln1_b, ffn_w_gate, ffn_w_up, ffn_w_down, ln2_g, ln2_b):
    bsz, seqlen, d = x.shape
    depth = mod_w.shape[0]
    n_groups = ssm_a_re.shape[1]
    d_ssm = n_groups * SSM_GROUP
    d_att = (w_in.shape[2] - d_ssm) // 3
    alpha = (2 * depth) ** 0.25

    mod = _modulation(c, mod_w, mod_b).reshape(depth, bsz, 6, d)
    ropes = _rope_tables(positions)
    lb_re, lb_im, bb_re, bb_im = _discretise(ssm_a_re, ssm_a_im, ssm_log_step, ssm_b_re, ssm_b_im)
    n_lane_rows = n_groups * SSM_STATE // LANES
    lam = jnp.stack([lb_re.reshape(depth, n_lane_rows, LANES),
                     lb_im.reshape(depth, n_lane_rows, LANES)], axis=1)
    wb = _block_diag_in(bb_re, bb_im)
    wc = _block_diag_out(ssm_c_re, ssm_c_im)
    d_skip = ssm_d.reshape(depth, 1, d_ssm)
    row = lambda a: a.reshape(depth, 1, a.shape[-1])
    subln_col = subln_w.reshape(depth, -1, 1)

    w_in_b = w_in.astype(BF16)
    glu_w_b = glu_w.astype(BF16)
    w_out_b = w_out.astype(BF16)
    wg_b = ffn_w_gate.astype(BF16)
    wu_b = ffn_w_up.astype(BF16)
    wd_b = ffn_w_down.astype(BF16)
    glu_b_r, lq1, lk1, lq2, lk2 = row(glu_b), row(lam_q1), row(lam_k1), row(lam_q2), row(lam_k2)
    g1, b1, g2, b2 = row(ln1_g), row(ln1_b), row(ln2_g), row(ln2_b)

    for l in range(depth):
        lam_init = 0.8 - 0.6 * math.exp(-0.3 * l)
        u, qt, k, vt = _in_proj(x, mod, w_in_b, ropes, l, d_ssm, d_att)
        ssm = _s5_mixer(u, lam, wb, wc, d_skip, glu_w_b, glu_b_r, l)
        att = _diff_attention(qt, k, vt, lq1, lk1, lq2, lk2, subln_col, l, lam_init)
        x = _mix_ffn(x, ssm, att, mod, w_out_b, g1, b1, wg_b, wu_b, wd_b, g2, b2, l, alpha)
    return x
```
